```python
import math
import jax, jax.numpy as jnp
from jax import lax
import numpy as np

D_MODEL = 2048
BATCH = 8
SEQ = 4096
DEPTH = 4

CHUNK = 64
Q_BLOCK = 128
N_BRANCH = 4
BRANCH_WIDTH = D_MODEL // 4
DIFF_HEADS = 4
DIFF_QK_DIM = BRANCH_WIDTH // (2 * DIFF_HEADS)
DIFF_V_DIM = BRANCH_WIDTH // DIFF_HEADS
GLA_HEADS = 4
GLA_DK = BRANCH_WIDTH // (2 * GLA_HEADS)
GLA_DV = BRANCH_WIDTH // GLA_HEADS
GLA_GATE_RANK = 16
GLA_GATE_NORM = 16.0
MLSTM_HEADS = 4
MLSTM_DH = BRANCH_WIDTH // MLSTM_HEADS
MLSTM_CONV = 4
CA_HEADS = 4
CA_DH = BRANCH_WIDTH // CA_HEADS
CA_LEFT_CHUNKS = 8
REL_CLIP = 2 * CHUNK
D_FF = 4 * D_MODEL
EPS = 1e-6
DIFF_COLS = 3 * BRANCH_WIDTH
GLA_COLS = 2 * GLA_HEADS * GLA_DK + BRANCH_WIDTH + GLA_GATE_RANK + BRANCH_WIDTH
MLSTM_COLS = 4 * BRANCH_WIDTH + 2 * MLSTM_HEADS
CA_COLS = 3 * BRANCH_WIDTH
N_IN = DIFF_COLS + GLA_COLS + MLSTM_COLS + CA_COLS

kernel_name = 'hybrid_gated_streaming_encoder'


def rms_norm(x, g):
    xf = x.astype(jnp.float32)
    y = xf * lax.rsqrt(jnp.mean(xf * xf, axis=-1, keepdims=True) + EPS)
    return (y * g.astype(jnp.float32)).astype(x.dtype)


def head_rms_norm(x, g):
    H, d = x.shape[-2], x.shape[-1]
    return rms_norm(x, g.reshape(H, d))


def alibi_slopes(n):
    return jnp.asarray([2.0 ** (-8.0 * (h + 1) / n) for h in range(n)], jnp.float32)


def causal_conv(x, w, b):
    K, C = w.shape
    y = lax.conv_general_dilated(x, w[:, None, :], window_strides=(1,), padding=[(K - 1, 0)],
                                 dimension_numbers=('NWC', 'WIO', 'NWC'), feature_group_count=C)
    return y + b


def diff_attention(u, lam, gain, layer_idx):
    B, T, _ = u.shape
    q, k, v = jnp.split(u, 3, axis=-1)
    q = q.reshape(B, T, DIFF_HEADS, 2, DIFF_QK_DIM)
    k = k.reshape(B, T, DIFF_HEADS, 2, DIFF_QK_DIM)
    v = v.reshape(B, T, DIFF_HEADS, DIFF_V_DIM)
    lam_init = 0.8 - 0.6 * math.exp(-0.3 * layer_idx)
    lf = lam.astype(jnp.float32)
    lam_full = jnp.exp(jnp.sum(lf[0] * lf[1])) - jnp.exp(jnp.sum(lf[2] * lf[3])) + lam_init
    slopes = alibi_slopes(DIFF_HEADS)[None, :, None, None, None]
    scale = DIFF_QK_DIM ** -0.5
    pos = jnp.arange(T)
    outs = []
    for blk in range(T // Q_BLOCK):
        q0, q1 = blk * Q_BLOCK, (blk + 1) * Q_BLOCK
        tq, tk = pos[q0:q1], pos[:q1]
        s = jnp.einsum('bqhmd,bkhmd->bhmqk', q[:, q0:q1], k[:, :q1]).astype(jnp.float32) * scale
        dist = jnp.abs(tq[:, None] - tk[None, :]).astype(jnp.float32)
        allowed = (tk[None, :] // CHUNK) <= (tq[:, None] // CHUNK)
        s = jnp.where(allowed, s - slopes * dist, -jnp.inf)
        p = jax.nn.softmax(s, axis=-1)
        a = p[:, :, 0] - lam_full * p[:, :, 1]
        outs.append(jnp.einsum('bhqk,bkhd->bqhd', a.astype(v.dtype), v[:, :q1]))
    o = jnp.concatenate(outs, axis=1)
    o = head_rms_norm(o, gain) * (1.0 - lam_init)
    return o.reshape(B, T, -1)


def gla(u, w_gate_up, b_gate, gain):
    B, T, _ = u.shape
    H, dk, dv, L = GLA_HEADS, GLA_DK, GLA_DV, CHUNK
    nc = T // L
    c1 = H * dk
    q, k, v, g_low, r = jnp.split(u, [c1, 2 * c1, 2 * c1 + BRANCH_WIDTH, 2 * c1 + BRANCH_WIDTH + GLA_GATE_RANK], axis=-1)
    log_a = jax.nn.log_sigmoid((g_low @ w_gate_up + b_gate).astype(jnp.float32)) / GLA_GATE_NORM
    log_a = log_a.reshape(B, nc, L, H, dk)
    q = q.reshape(B, nc, L, H, dk).astype(jnp.float32) * dk ** -0.5
    k = k.reshape(B, nc, L, H, dk).astype(jnp.float32)
    v = v.reshape(B, nc, L, H, dv).astype(jnp.float32)
    cum = jnp.cumsum(log_a, axis=2)
    total = cum[:, :, -1]
    q_dec = q * jnp.exp(cum)
    k_inv = k * jnp.exp(-cum)
    k_tail = k * jnp.exp(total[:, :, None] - cum)
    causal = jnp.tril(jnp.ones((L, L), dtype=bool))
    att = jnp.where(causal, jnp.einsum('bcthk,bcshk->bchts', q_dec, k_inv), 0.0)
    o_intra = jnp.einsum('bchts,bcshv->bcthv', att, v)
    dS = jnp.einsum('bcshk,bcshv->bchkv', k_tail, v)

    def step(S, inp):
        decay, inc = inp
        return jnp.exp(decay)[..., None] * S + inc, S

    S0 = jnp.zeros((B, H, dk, dv), jnp.float32)
    _, S_prev = lax.scan(step, S0, (jnp.moveaxis(total, 1, 0), jnp.moveaxis(dS, 1, 0)))
    S_prev = jnp.moveaxis(S_prev, 0, 1)
    o_inter = jnp.einsum('bcthk,bchkv->bcthv', q_dec, S_prev)
    o = (o_intra + o_inter).reshape(B, T, H, dv)
    o = head_rms_norm(o, gain).reshape(B, T, -1) * jax.nn.silu(r.astype(jnp.float32))
    return o.astype(u.dtype)


def mlstm(u, conv_w, conv_b, b_i, b_f, gain):
    B, T, _ = u.shape
    H, d, L, W = MLSTM_HEADS, MLSTM_DH, CHUNK, BRANCH_WIDTH
    nc = T // L
    qk, v, o_pre, i_pre, f_pre = jnp.split(u, [2 * W, 3 * W, 4 * W, 4 * W + H], axis=-1)
    qk = jax.nn.silu(causal_conv(qk, conv_w, conv_b))
    q, k = jnp.split(qk, 2, axis=-1)
    hm = lambda a: a.reshape(B, nc, L, H, d).astype(jnp.float32).transpose(0, 1, 3, 2, 4)
    q, k, v = hm(q), hm(k) * d ** -0.5, hm(v)
    log_i = jnp.swapaxes((i_pre.astype(jnp.float32) + b_i).reshape(B, nc, L, H), 2, 3)
    log_f = jnp.swapaxes(jax.nn.log_sigmoid(f_pre.astype(jnp.float32) + b_f).reshape(B, nc, L, H), 2, 3)
    bcum = jnp.cumsum(log_f, axis=-1)
    btot = bcum[..., -1]
    causal = jnp.tril(jnp.ones((L, L), dtype=bool))
    Dlog = jnp.where(causal, bcum[..., :, None] - bcum[..., None, :] + log_i[..., None, :], -jnp.inf)
    tail = btot[..., None] - bcum + log_i
    m_loc = jnp.max(tail, axis=-1)
    w_tail = jnp.exp(tail - m_loc[..., None])
    dC = jnp.einsum('bchs,bchsk,bchsv->bchkv', w_tail, k, v)
    dn = jnp.einsum('bchs,bchsk->bchk', w_tail, k)

    def step(carry, inp):
        C, n, m = carry
        g, ml, dCc, dnc = inp
        m_new = jnp.maximum(g + m, ml)
        a, bb = jnp.exp(g + m - m_new), jnp.exp(ml - m_new)
        C_new = a[..., None, None] * C + bb[..., None, None] * dCc
        n_new = a[..., None] * n + bb[..., None] * dnc
        return (C_new, n_new, m_new), (C, n, m)

    init = (jnp.zeros((B, H, d, d), jnp.float32), jnp.zeros((B, H, d), jnp.float32), jnp.zeros((B, H), jnp.float32))
    mv = lambda a: jnp.moveaxis(a, 1, 0)
    _, (C_prev, n_prev, m_prev) = lax.scan(step, init, (mv(btot), mv(m_loc), mv(dC), mv(dn)))
    C_prev, n_prev, m_prev = jnp.moveaxis(C_prev, 0, 1), jnp.moveaxis(n_prev, 0, 1), jnp.moveaxis(m_prev, 0, 1)
    inter_log = bcum + m_prev[..., None]
    m_t = jnp.maximum(jnp.max(Dlog, axis=-1), inter_log)
    A = jnp.exp(Dlog - m_t[..., None]) * jnp.einsum('bchtk,bchsk->bchts', q, k)
    inter_w = jnp.exp(inter_log - m_t)
    num = jnp.einsum('bchts,bchsv->bchtv', A, v) + inter_w[..., None] * jnp.einsum('bchtk,bchkv->bchtv', q, C_prev)
    den = jnp.sum(A, axis=-1) + inter_w * jnp.einsum('bchtk,bchk->bcht', q, n_prev)
    h = num / jnp.maximum(jnp.abs(den), jnp.exp(-m_t))[..., None]
    h = h.transpose(0, 1, 3, 2, 4).reshape(B, T, H, d)
    o_gate = jax.nn.sigmoid(o_pre.astype(jnp.float32)).reshape(B, T, H, d)
    return head_rms_norm(o_gate * h, gain).reshape(B, T, -1).astype(u.dtype)


def chunk_band_attention(u, rel_table):
    B, T, _ = u.shape
    H, d, L = CA_HEADS, CA_DH, CHUNK
    nc, nb = T // L, CA_LEFT_CHUNKS + 1
    q, k, v = jnp.split(u, 3, axis=-1)
    q = q.reshape(B, nc, L, H, d)
    pad = ((0, 0), (CA_LEFT_CHUNKS, 0), (0, 0), (0, 0), (0, 0))
    kp = jnp.pad(k.reshape(B, nc, L, H, d), pad)
    vp = jnp.pad(v.reshape(B, nc, L, H, d), pad)
    band_idx = jnp.arange(nc)[:, None] + jnp.arange(nb)[None, :]
    kb = kp[:, band_idx].reshape(B, nc, nb * L, H, d)
    vb = vp[:, band_idx].reshape(B, nc, nb * L, H, d)
    s = jnp.einsum('bcqhd,bckhd->bchqk', q, kb).astype(jnp.float32) * d ** -0.5
    i = jnp.arange(L)
    j = jnp.arange(nb)
    key_off = ((nb - 1 - j)[:, None] * L - i[None, :]).reshape(-1)
    rel = i[:, None] + key_off[None, :]
    bias = rel_table[:, jnp.clip(rel, -REL_CLIP, REL_CLIP) + REL_CLIP].astype(jnp.float32)
    valid = jnp.repeat(band_idx >= CA_LEFT_CHUNKS, L, axis=1)
    s = jnp.where(valid[None, :, None, None, :], s + bias[None, None], -jnp.inf)
    p = jax.nn.softmax(s, axis=-1)
    o = jnp.einsum('bchqk,bckhd->bcqhd', p.astype(vb.dtype), vb)
    return o.reshape(B, T, H * d)


def setup_inputs(seed: int = 0) -> dict:
    key = jax.random.key(seed)
    ks = jax.random.split(key, 24)
    f32 = jnp.float32
    nrm = lambda k, shape, scale: scale * jax.random.normal(k, shape, f32)
    gain = lambda k, shape: 1.0 + 0.05 * jax.random.normal(k, shape, f32)
    L, D, W = DEPTH, D_MODEL, BRANCH_WIDTH
    return {
        'x': nrm(ks[0], (BATCH, SEQ, D), 1.0),
        'norm_mix_pre': gain(ks[1], (L, D)),
        'norm_mix_post': gain(ks[2], (L, D)),
        'w_in': nrm(ks[3], (L, D, N_IN), D ** -0.5),
        'diff_lambda': nrm(ks[4], (L, 4, DIFF_QK_DIM), 0.1),
        'diff_norm': gain(ks[5], (L, W)),
        'gla_w_gate_up': nrm(ks[6], (L, GLA_GATE_RANK, GLA_HEADS * GLA_DK), GLA_GATE_RANK ** -0.5),
        'gla_b_gate': nrm(ks[7], (L, GLA_HEADS * GLA_DK), 0.1),
        'gla_norm': gain(ks[8], (L, W)),
        'mlstm_conv_w': nrm(ks[9], (L, MLSTM_CONV, 2 * W), MLSTM_CONV ** -0.5),
        'mlstm_conv_b': nrm(ks[10], (L, 2 * W), 0.01),
        'mlstm_b_i': nrm(ks[11], (L, MLSTM_HEADS), 0.1),
        'mlstm_b_f': jnp.linspace(3.0, 6.0, MLSTM_HEADS, dtype=f32)[None, :] + nrm(ks[12], (L, MLSTM_HEADS), 0.1),
        'mlstm_norm': gain(ks[13], (L, W)),
        'rel_bias': nrm(ks[14], (L, CA_HEADS, 2 * REL_CLIP + 1), 0.1),
        'w_branch': nrm(ks[15], (L, N_BRANCH, W, D), W ** -0.5),
        'w_gate': nrm(ks[16], (L, N_BRANCH, D, D), D ** -0.5),
        'b_gate': nrm(ks[17], (L, N_BRANCH, D), 0.01),
        'w_out': nrm(ks[18], (L, D, D), D ** -0.5),
        'norm_ffn_pre': gain(ks[19], (L, D)),
        'norm_ffn_post': gain(ks[20], (L, D)),
        'w_up': nrm(ks[21], (L, D, D_FF), D ** -0.5),
        'w_down': nrm(ks[22], (L, D_FF, D), D_FF ** -0.5),
    }


def reference(x, norm_mix_pre, norm_mix_post, w_in, diff_lambda, diff_norm, gla_w_gate_up, gla_b_gate,
              gla_norm, mlstm_conv_w, mlstm_conv_b, mlstm_b_i, mlstm_b_f, mlstm_norm, rel_bias, w_branch,
              w_gate, b_gate, w_out, norm_ffn_pre, norm_ffn_post, w_up, w_down):
    for l in range(DEPTH):
        h = rms_norm(x, norm_mix_pre[l])
        u = h @ w_in[l]
        u_a, u_b, u_c, u_d = jnp.split(u, [DIFF_COLS, DIFF_COLS + GLA_COLS, DIFF_COLS + GLA_COLS + MLSTM_COLS], axis=-1)
        branches = [
            diff_attention(u_a, diff_lambda[l], diff_norm[l], l),
            gla(u_b, gla_w_gate_up[l], gla_b_gate[l], gla_norm[l]),
            mlstm(u_c, mlstm_conv_w[l], mlstm_conv_b[l], mlstm_b_i[l], mlstm_b_f[l], mlstm_norm[l]),
            chunk_band_attention(u_d, rel_bias[l]),
        ]
        merged = None
        for bi in range(N_BRANCH):
            gate = jax.nn.sigmoid(h @ w_gate[l, bi] + b_gate[l, bi])
            term = gate * (branches[bi] @ w_branch[l, bi])
            merged = term if merged is None else merged + term
        x = x + rms_norm(merged @ w_out[l], norm_mix_post[l])
        h2 = rms_norm(x, norm_ffn_pre[l])
        f = jnp.square(jax.nn.relu(h2 @ w_up[l])) @ w_down[l]
        x = x + rms_norm(f, norm_ffn_post[l])
    return x
```

```python
import functools
import math

import jax
import jax.numpy as jnp
from jax import lax
from jax.experimental import pallas as pl
from jax.experimental.pallas import tpu as pltpu

F32 = jnp.float32
BF16 = jnp.bfloat16

D_MODEL = 2048
CHUNK = 64
BRANCH_WIDTH = 512
N_HEADS = 4
HEAD_DV = 128
DIFF_QK_DIM = 64
GLA_DK = 64
GLA_GATE_RANK = 16
GLA_GATE_NORM = 16.0
MLSTM_CONV = 4
CA_LEFT_CHUNKS = 8
REL_CLIP = 2 * CHUNK
D_FF = 4 * D_MODEL
EPS = 1e-6
NEG = -1e30

LANES = 128
VMEM_LIMIT = 56 * 1024 * 1024

MM_DIFF_Q, MM_DIFF_K, MM_DIFF_V = 0, 512, 1024
MM_GLA_Q, MM_GLA_K, MM_GLA_V = 1536, 1792, 2048
MM_ML_V = 2560
MM_CA_Q, MM_CA_K, MM_CA_V = 3072, 3584, 4096
MM_COLS = 4608
FX_ML_QK, FX_GLA_R, FX_ML_O, FX_GATES = 0, 1024, 1536, 2048
FX_COLS = 2176
GATE_GLA_LOW, GATE_ML_I, GATE_ML_F = 0, 16, 20

NT_DIMS = (((1,), (1,)), ((), ()))
TN_DIMS = (((0,), (0,)), ((), ()))


def _nt(a, b):
    return lax.dot_general(a, b, NT_DIMS, preferred_element_type=F32)


def _tn(a, b):
    return lax.dot_general(a, b, TN_DIMS, preferred_element_type=F32)


def _mm(a, b):
    return jnp.dot(a, b, preferred_element_type=F32)


def _mm_exact(a, b):
    return jnp.dot(a, b, preferred_element_type=F32, precision=lax.Precision.HIGHEST)


def _rms(x, gain):
    return x * lax.rsqrt(jnp.mean(x * x, axis=-1, keepdims=True) + EPS) * gain


def _log_sigmoid(x):
    return jnp.minimum(x, 0.0) - jnp.log1p(jnp.exp(-jnp.abs(x)))


def _sigmoid(x):
    return 1.0 / (1.0 + jnp.exp(-x))


def _params(*sem):
    return pltpu.CompilerParams(dimension_semantics=sem, vmem_limit_bytes=VMEM_LIMIT)


def _rmsnorm_cast_kernel(x_ref, g_ref, o_ref):
    o_ref[...] = _rms(x_ref[...], g_ref[...]).astype(o_ref.dtype)


def rmsnorm_cast(x, g, tm=512):
    m, d = x.shape
    return pl.pallas_call(
        _rmsnorm_cast_kernel,
        grid=(m // tm,),
        in_specs=[pl.BlockSpec((tm, d), lambda i: (i, 0)), pl.BlockSpec((1, d), lambda i: (0, 0))],
        out_specs=pl.BlockSpec((tm, d), lambda i: (i, 0)),
        out_shape=jax.ShapeDtypeStruct((m, d), BF16),
        compiler_params=_params("parallel"),
        name="rmsnorm_cast",
    )(x, g.reshape(1, d))


def _matmul_kernel(a_ref, b_ref, o_ref):
    o_ref[...] = _mm(a_ref[...], b_ref[...]).astype(o_ref.dtype)


def matmul(a, b, out_dtype, tm, tn, name):
    m, k = a.shape
    n = b.shape[1]
    return pl.pallas_call(
        _matmul_kernel,
        grid=(m // tm, n // tn),
        in_specs=[pl.BlockSpec((tm, k), lambda i, j: (i, 0)), pl.BlockSpec((k, tn), lambda i, j: (0, j))],
        out_specs=pl.BlockSpec((tm, tn), lambda i, j: (i, j)),
        out_shape=jax.ShapeDtypeStruct((m, n), out_dtype),
        compiler_params=_params("parallel", "arbitrary"),
        name=name,
    )(a, b)


def _diff_attn_kernel(slopes_ref, lam_ref, gain_ref, q_ref, k_ref, v_ref, o_ref, *, tq, tk, lam_init):
    h = pl.program_id(1)
    qi = pl.program_id(2)
    slope = slopes_ref[h]
    q0 = qi * tq
    lanes = lax.broadcasted_iota(jnp.int32, (1, 2 * DIFF_QK_DIM), 1)
    q = q_ref[...] * jnp.asarray(DIFF_QK_DIM ** -0.5, BF16)
    qmaps = (jnp.where(lanes < DIFF_QK_DIM, q, jnp.zeros_like(q)),
             jnp.where(lanes >= DIFF_QK_DIM, q, jnp.zeros_like(q)))
    row = lax.broadcasted_iota(jnp.int32, (tq, tk), 0)
    col = lax.broadcasted_iota(jnp.int32, (tq, tk), 1)
    rel = row - col
    col_minus_limit = col - (row // CHUNK + 1) * CHUNK

    def body(j, carry):
        k0 = pl.multiple_of(j * tk, tk)
        kt = k_ref[pl.ds(k0, tk), :]
        vt = v_ref[pl.ds(k0, tk), :]
        dist = jnp.abs(rel + (q0 - k0)).astype(F32)
        allowed = col_minus_limit < q0 - k0
        bias = jnp.where(allowed, -slope * dist, NEG)
        out = []
        for mi in range(2):
            m_prev, l_prev, acc = carry[3 * mi:3 * mi + 3]
            s = _nt(qmaps[mi], kt) + bias
            m_new = jnp.maximum(m_prev, jnp.max(s, axis=-1, keepdims=True))
            alpha = jnp.exp(m_prev - m_new)
            p = jnp.exp(s - m_new)
            l_new = alpha * l_prev + jnp.sum(p, axis=-1, keepdims=True)
            acc = alpha * acc + _mm(p.astype(BF16), vt)
            out += [m_new, l_new, acc]
        return tuple(out)

    m_init = jnp.full((tq, 1), NEG, F32)
    l_init = jnp.zeros((tq, 1), F32)
    a_init = jnp.zeros((tq, HEAD_DV), F32)
    n_k = (q0 + tq + tk - 1) // tk
    m0, l0, a0, m1, l1, a1 = lax.fori_loop(0, n_k, body, (m_init, l_init, a_init) * 2)

    lam = lam_ref[...]
    lam_full = (jnp.exp(jnp.sum(lam[0:1] * lam[1:2], axis=-1, keepdims=True))
                - jnp.exp(jnp.sum(lam[2:3] * lam[3:4], axis=-1, keepdims=True)) + lam_init)
    o = a0 / l0 - lam_full * (a1 / l1)
    o_ref[...] = (_rms(o, gain_ref[...]) * (1.0 - lam_init)).astype(o_ref.dtype)


def diff_attention(u_mm, lam, gain, layer_idx, batch, seq, tq=256, tk=512):
    lam_init = 0.8 - 0.6 * math.exp(-0.3 * layer_idx)
    nq = seq // tq
    slopes = jnp.asarray([2.0 ** (-8.0 * (i + 1) / N_HEADS) for i in range(N_HEADS)], F32)
    cq, ck, cv = MM_DIFF_Q // LANES, MM_DIFF_K // LANES, MM_DIFF_V // LANES
    kern = functools.partial(_diff_attn_kernel, tq=tq, tk=tk, lam_init=lam_init)
    return pl.pallas_call(
        kern,
        grid=(batch, N_HEADS, nq),
        in_specs=[
            pl.BlockSpec(memory_space=pltpu.SMEM),
            pl.BlockSpec((4, DIFF_QK_DIM), lambda b, h, i: (0, 0)),
            pl.BlockSpec((1, HEAD_DV), lambda b, h, i: (0, h)),
            pl.BlockSpec((tq, LANES), lambda b, h, i: (b * nq + i, cq + h)),
            pl.BlockSpec((seq, LANES), lambda b, h, i: (b, ck + h)),
            pl.BlockSpec((seq, LANES), lambda b, h, i: (b, cv + h)),
        ],
        out_specs=pl.BlockSpec((tq, HEAD_DV), lambda b, h, i: (b * nq + i, h)),
        out_shape=jax.ShapeDtypeStruct((batch * seq, BRANCH_WIDTH), BF16),
        compiler_params=_params("parallel", "parallel", "arbitrary"),
        name="diff_attention",
    )(slopes, lam, gain.reshape(1, BRANCH_WIDTH), u_mm, u_mm, u_mm)


def _gla_kernel(q_ref, k_ref, v_ref, r_ref, g_ref, wgu_ref, bg_ref, gain_ref, o_ref, st_ref, *, tt):
    @pl.when(pl.program_id(1) == 0)
    def _():
        st_ref[...] = jnp.zeros_like(st_ref)

    row = lax.broadcasted_iota(jnp.int32, (CHUNK, CHUNK), 0)
    col = lax.broadcasted_iota(jnp.int32, (CHUNK, CHUNK), 1)
    causal = row >= col
    tril = causal.astype(F32)

    def chunk(c, _):
        r0 = pl.multiple_of(c * CHUNK, CHUNK)
        rows = pl.ds(r0, CHUNK)
        pre = _mm(g_ref[rows, :].astype(BF16), wgu_ref[...]) + bg_ref[...]
        log_a = _log_sigmoid(pre) * (1.0 / GLA_GATE_NORM)
        cum = _mm_exact(tril, log_a)
        total = cum[CHUNK - 1:CHUNK, :]
        qf = q_ref[rows, :].astype(F32) * (GLA_DK ** -0.5)
        kf = k_ref[rows, :].astype(F32)
        q_dec = (qf * jnp.exp(cum)).astype(BF16)
        k_inv = (kf * jnp.exp(-cum)).astype(BF16)
        k_tail = (kf * jnp.exp(total - cum)).astype(BF16)
        decay = jnp.exp(total)
        st = st_ref[...]
        for h in range(N_HEADS):
            ks = slice(h * GLA_DK, (h + 1) * GLA_DK)
            vs = slice(h * HEAD_DV, (h + 1) * HEAD_DV)
            vh = v_ref[rows, vs]
            att = jnp.where(causal, _nt(q_dec[:, ks], k_inv[:, ks]), 0.0)
            o = _mm(att.astype(BF16), vh) + _nt(q_dec[:, ks], st[:, ks].astype(BF16))
            st_ref[:, ks] = decay[:, ks] * st[:, ks] + _tn(vh, k_tail[:, ks])
            r = r_ref[rows, vs]
            y = _rms(o, gain_ref[:, vs]) * (r * _sigmoid(r))
            o_ref[rows, vs] = y.astype(o_ref.dtype)
        return 0

    lax.fori_loop(0, tt // CHUNK, chunk, 0)


def gla(u_mm, u_fx, w_gate_up, b_gate, gain, batch, seq, tt=1024):
    dkh = N_HEADS * GLA_DK
    tt = min(tt, seq)
    nt = seq // tt
    wgu = jnp.zeros((LANES, dkh), F32).at[GATE_GLA_LOW:GATE_GLA_LOW + GLA_GATE_RANK].set(w_gate_up).astype(BF16)
    kern = functools.partial(_gla_kernel, tt=tt)
    return pl.pallas_call(
        kern,
        grid=(batch, nt),
        in_specs=[
            pl.BlockSpec((tt, dkh), lambda b, t: (b * nt + t, MM_GLA_Q // dkh)),
            pl.BlockSpec((tt, dkh), lambda b, t: (b * nt + t, MM_GLA_K // dkh)),
            pl.BlockSpec((tt, BRANCH_WIDTH), lambda b, t: (b * nt + t, MM_GLA_V // BRANCH_WIDTH)),
            pl.BlockSpec((tt, BRANCH_WIDTH), lambda b, t: (b * nt + t, FX_GLA_R // BRANCH_WIDTH)),
            pl.BlockSpec((tt, LANES), lambda b, t: (b * nt + t, FX_GATES // LANES)),
            pl.BlockSpec((LANES, dkh), lambda b, t: (0, 0)),
            pl.BlockSpec((1, dkh), lambda b, t: (0, 0)),
            pl.BlockSpec((1, BRANCH_WIDTH), lambda b, t: (0, 0)),
        ],
        out_specs=pl.BlockSpec((tt, BRANCH_WIDTH), lambda b, t: (b * nt + t, 0)),
        out_shape=jax.ShapeDtypeStruct((batch * seq, BRANCH_WIDTH), BF16),
        scratch_shapes=[pltpu.VMEM((HEAD_DV, dkh), F32)],
        compiler_params=_params("parallel", "arbitrary"),
        name="gla",
    )(u_mm, u_mm, u_mm, u_fx, u_fx, wgu, b_gate.reshape(1, dkh), gain.reshape(1, BRANCH_WIDTH))


def _mlstm_kernel(qk_ref, v_ref, op_ref, g_ref, cw_ref, cb_ref, gb_ref, gain_ref, o_ref,
                  c_ref, n_ref, m_ref, halo_ref, *, tt):
    w = BRANCH_WIDTH
    halo = halo_ref.shape[0]

    @pl.when(pl.program_id(1) == 0)
    def _():
        c_ref[...] = jnp.zeros_like(c_ref)
        n_ref[...] = jnp.zeros_like(n_ref)
        m_ref[...] = jnp.zeros_like(m_ref)
        halo_ref[...] = jnp.zeros_like(halo_ref)

    row = lax.broadcasted_iota(jnp.int32, (CHUNK, CHUNK), 0)
    col = lax.broadcasted_iota(jnp.int32, (CHUNK, CHUNK), 1)
    causal = row >= col
    tril = causal.astype(F32)
    lane = lax.broadcasted_iota(jnp.int32, (1, LANES), 1)

    def chunk(c, _):
        r0 = pl.multiple_of(c * CHUNK, CHUNK)
        rows = pl.ds(r0, CHUNK)
        p0 = pl.multiple_of(jnp.maximum(r0 - halo, 0), halo)
        prev = jnp.where(c > 0, qk_ref[pl.ds(p0, halo), :], halo_ref[...])
        ext = jnp.concatenate([prev, qk_ref[rows, :]], axis=0)
        acc = cb_ref[...]
        for j in range(MLSTM_CONV):
            s0 = halo - (MLSTM_CONV - 1) + j
            acc = acc + cw_ref[j:j + 1, :] * ext[s0:s0 + CHUNK, :]
        qk = acc * _sigmoid(acc)
        gb = g_ref[rows, :] + gb_ref[...]
        log_f = _log_sigmoid(gb)
        bcum = _mm_exact(tril, log_f)
        x = jnp.where(lane < GATE_ML_F, gb, bcum)
        xt = x.T
        for h in range(N_HEADS):
            hs = slice(h * HEAD_DV, (h + 1) * HEAD_DV)
            li, lf = GATE_ML_I + h, GATE_ML_F + h
            q = qk[:, h * HEAD_DV:(h + 1) * HEAD_DV].astype(BF16)
            kf = qk[:, w + h * HEAD_DV:w + (h + 1) * HEAD_DV] * (HEAD_DV ** -0.5)
            k = kf.astype(BF16)
            v = v_ref[rows, hs]
            bcum_c = bcum[:, lf:lf + 1]
            logi_c = gb[:, li:li + 1]
            btot = bcum_c[CHUNK - 1:CHUNK, :]
            r_row = xt[li:li + 1, :] - xt[lf:lf + 1, :]
            dlog = jnp.where(causal, bcum_c + r_row, NEG)
            tail = btot - bcum_c + logi_c
            m_loc = jnp.max(tail, axis=0, keepdims=True)
            w_tail = jnp.exp(tail - m_loc)
            kw = kf * w_tail
            d_c = _tn(kw.astype(BF16), v)
            d_n = jnp.sum(kw, axis=0, keepdims=True)
            c_prev = c_ref[h]
            n_prev = n_ref[h]
            m_prev = m_ref[h]
            inter_log = bcum_c + m_prev
            m_t = jnp.maximum(jnp.max(dlog, axis=-1, keepdims=True), inter_log)
            a_mat = jnp.exp(dlog - m_t) * _nt(q, k)
            inter_w = jnp.exp(inter_log - m_t)
            num = _mm(a_mat.astype(BF16), v) + inter_w * _mm(q, c_prev.astype(BF16))
            qn = jnp.sum(q.astype(F32) * n_prev, axis=-1, keepdims=True)
            den = jnp.sum(a_mat, axis=-1, keepdims=True) + inter_w * qn
            hh = num / jnp.maximum(jnp.abs(den), jnp.exp(-m_t))
            y = _rms(_sigmoid(op_ref[rows, hs]) * hh, gain_ref[:, hs])
            o_ref[rows, hs] = y.astype(o_ref.dtype)
            m_new = jnp.maximum(btot + m_prev, m_loc)
            a = jnp.exp(btot + m_prev - m_new)
            bb = jnp.exp(m_loc - m_new)
            c_ref[h] = a * c_prev + bb * d_c
            n_ref[h] = a * n_prev + bb * d_n
            m_ref[h] = m_new
        return 0

    lax.fori_loop(0, tt // CHUNK, chunk, 0)
    halo_ref[...] = qk_ref[tt - halo:tt, :]


def mlstm(u_mm, u_fx, conv_w, conv_b, b_i, b_f, gain, batch, seq, tt=1024):
    w = BRANCH_WIDTH
    tt = min(tt, seq)
    nt = seq // tt
    gate_bias = (jnp.zeros((1, LANES), F32).at[0, GATE_ML_I:GATE_ML_I + N_HEADS].set(b_i)
                 .at[0, GATE_ML_F:GATE_ML_F + N_HEADS].set(b_f))
    kern = functools.partial(_mlstm_kernel, tt=tt)
    const = lambda b, t: (0, 0)
    return pl.pallas_call(
        kern,
        grid=(batch, nt),
        in_specs=[
            pl.BlockSpec((tt, 2 * w), lambda b, t: (b * nt + t, FX_ML_QK // (2 * w))),
            pl.BlockSpec((tt, w), lambda b, t: (b * nt + t, MM_ML_V // w)),
            pl.BlockSpec((tt, w), lambda b, t: (b * nt + t, FX_ML_O // w)),
            pl.BlockSpec((tt, LANES), lambda b, t: (b * nt + t, FX_GATES // LANES)),
            pl.BlockSpec((MLSTM_CONV, 2 * w), const),
            pl.BlockSpec((1, 2 * w), const),
            pl.BlockSpec((1, LANES), const),
            pl.BlockSpec((1, w), const),
        ],
        out_specs=pl.BlockSpec((tt, w), lambda b, t: (b * nt + t, 0)),
        out_shape=jax.ShapeDtypeStruct((batch * seq, w), BF16),
        scratch_shapes=[pltpu.VMEM((N_HEADS, HEAD_DV, HEAD_DV), F32),
                        pltpu.VMEM((N_HEADS, 1, HEAD_DV), F32),
                        pltpu.VMEM((N_HEADS, 1, 1), F32),
                        pltpu.VMEM((8, 2 * w), F32)],
        compiler_params=_params("parallel", "arbitrary"),
        name="mlstm",
    )(u_fx, u_mm, u_fx, u_fx, conv_w, conv_b.reshape(1, 2 * w), gate_bias, gain.reshape(1, w))


BAND_Q_CHUNKS = 4
BAND_TQ = BAND_Q_CHUNKS * CHUNK
BAND_TK = (CA_LEFT_CHUNKS + BAND_Q_CHUNKS) * CHUNK
BAND_PAD = CA_LEFT_CHUNKS * CHUNK


def _band_bias_kernel(table_ref, o_ref):
    l = pl.program_id(0)
    h = pl.program_id(1)
    row = lax.broadcasted_iota(jnp.int32, (BAND_TQ, BAND_TK), 0)
    col = lax.broadcasted_iota(jnp.int32, (BAND_TQ, BAND_TK), 1)
    idx = jnp.clip(row - col + BAND_PAD, -REL_CLIP, REL_CLIP) + REL_CLIP
    delta = row // CHUNK + CA_LEFT_CHUNKS - col // CHUNK
    in_band = (delta >= 0) & (delta <= CA_LEFT_CHUNKS)

    def body(r, acc):
        return jnp.where(idx == r, table_ref[l, h, r], acc)

    bias = lax.fori_loop(0, 2 * REL_CLIP + 1, body, jnp.zeros((BAND_TQ, BAND_TK), F32))
    o_ref[0, 0] = jnp.where(in_band, bias, NEG)


def band_bias(rel_bias):
    depth = rel_bias.shape[0]
    return pl.pallas_call(
        _band_bias_kernel,
        grid=(depth, N_HEADS),
        in_specs=[pl.BlockSpec(memory_space=pltpu.SMEM)],
        out_specs=pl.BlockSpec((1, 1, BAND_TQ, BAND_TK), lambda l, h: (l, h, 0, 0)),
        out_shape=jax.ShapeDtypeStruct((depth, N_HEADS, BAND_TQ, BAND_TK), F32),
        compiler_params=_params("parallel", "parallel"),
        name="band_bias",
    )(rel_bias)


def _band_attn_kernel(bias_ref, q_ref, k_ref, v_ref, o_ref, kp_ref, vp_ref, *, seq):
    kp_ref[0:BAND_PAD, :] = jnp.zeros((BAND_PAD, HEAD_DV), BF16)
    vp_ref[0:BAND_PAD, :] = jnp.zeros((BAND_PAD, HEAD_DV), BF16)
    kp_ref[BAND_PAD:BAND_PAD + seq, :] = k_ref[...]
    vp_ref[BAND_PAD:BAND_PAD + seq, :] = v_ref[...]
    col = lax.broadcasted_iota(jnp.int32, (BAND_TQ, BAND_TK), 1)
    scale = HEAD_DV ** -0.5

    def tile(t, _):
        r0 = pl.multiple_of(t * BAND_TQ, BAND_TQ)
        q = q_ref[pl.ds(r0, BAND_TQ), :]
        kb = kp_ref[pl.ds(r0, BAND_TK), :]
        vb = vp_ref[pl.ds(r0, BAND_TK), :]
        s = _nt(q, kb) * scale + bias_ref[0, 0]
        s = jnp.where(col + r0 >= BAND_PAD, s, NEG)
        p = jnp.exp(s - jnp.max(s, axis=-1, keepdims=True))
        o = _mm(p.astype(BF16), vb) / jnp.sum(p, axis=-1, keepdims=True)
        o_ref[pl.ds(r0, BAND_TQ), :] = o.astype(o_ref.dtype)
        return 0

    lax.fori_loop(0, seq // BAND_TQ, tile, 0)


def band_attention(u_mm, bias, layer, batch, seq):
    cq, ck, cv = MM_CA_Q // LANES, MM_CA_K // LANES, MM_CA_V // LANES
    kern = functools.partial(_band_attn_kernel, seq=seq)
    return pl.pallas_call(
        kern,
        grid=(batch, N_HEADS),
        in_specs=[
            pl.BlockSpec((1, 1, BAND_TQ, BAND_TK), lambda b, h: (layer, h, 0, 0)),
            pl.BlockSpec((seq, LANES), lambda b, h: (b, cq + h)),
            pl.BlockSpec((seq, LANES), lambda b, h: (b, ck + h)),
            pl.BlockSpec((seq, LANES), lambda b, h: (b, cv + h)),
        ],
        out_specs=pl.BlockSpec((seq, HEAD_DV), lambda b, h: (b, h)),
        out_shape=jax.ShapeDtypeStruct((batch * seq, BRANCH_WIDTH), BF16),
        scratch_shapes=[pltpu.VMEM((BAND_PAD + seq, HEAD_DV), BF16),
                        pltpu.VMEM((BAND_PAD + seq, HEAD_DV), BF16)],
        compiler_params=_params("parallel", "parallel"),
        name="band_attention",
    )(bias, u_mm, u_mm, u_mm)


def _merge_kernel(h_ref, ba_ref, bb_ref, bc_ref, bd_ref, x_ref, wg_ref, bg_ref, wb_ref, wo_ref,
                  gpost_ref, gnext_ref, xo_ref, ho_ref, acc_ref):
    j = pl.program_id(1)

    @pl.when(j == 0)
    def _():
        acc_ref[...] = jnp.zeros_like(acc_ref)

    h = h_ref[...]
    merged = None
    for bi, br_ref in enumerate((ba_ref, bb_ref, bc_ref, bd_ref)):
        gate = _sigmoid(_mm(h, wg_ref[bi]) + bg_ref[bi])
        term = gate * _mm(br_ref[...], wb_ref[bi])
        merged = term if merged is None else merged + term
    acc_ref[...] += _mm(merged.astype(BF16), wo_ref[...])

    @pl.when(j == pl.num_programs(1) - 1)
    def _():
        x_new = x_ref[...] + _rms(acc_ref[...], gpost_ref[...])
        xo_ref[...] = x_new
        ho_ref[...] = _rms(x_new, gnext_ref[...]).astype(ho_ref.dtype)


def merge(h, branches, x, w_gate, b_gate, w_branch, w_out, g_post, g_next, tm=512, tn=256):
    m, d = x.shape
    w = BRANCH_WIDTH
    row_d = pl.BlockSpec((tm, d), lambda i, j: (i, 0))
    row_w = pl.BlockSpec((tm, w), lambda i, j: (i, 0))
    vec_d = pl.BlockSpec((1, d), lambda i, j: (0, 0))
    return pl.pallas_call(
        _merge_kernel,
        grid=(m // tm, d // tn),
        in_specs=[
            row_d, row_w, row_w, row_w, row_w, row_d,
            pl.BlockSpec((4, d, tn), lambda i, j: (0, 0, j)),
            pl.BlockSpec((4, 1, tn), lambda i, j: (0, 0, j)),
            pl.BlockSpec((4, w, tn), lambda i, j: (0, 0, j)),
            pl.BlockSpec((tn, d), lambda i, j: (j, 0)),
            vec_d, vec_d,
        ],
        out_specs=[row_d, row_d],
        out_shape=[jax.ShapeDtypeStruct((m, d), F32), jax.ShapeDtypeStruct((m, d), BF16)],
        scratch_shapes=[pltpu.VMEM((tm, d), F32)],
        compiler_params=_params("parallel", "arbitrary"),
        name="merge",
    )(h, *branches, x, w_gate, b_gate.reshape(4, 1, d), w_branch, w_out, g_post.reshape(1, d), g_next.reshape(1, d))


def _ffn_kernel(h_ref, x_ref, wu_ref, wd_ref, gpost_ref, *rest, emit_next):
    if emit_next:
        gnext_ref, xo_ref, ho_ref, acc_ref = rest
    else:
        xo_ref, acc_ref = rest
    j = pl.program_id(1)

    @pl.when(j == 0)
    def _():
        acc_ref[...] = jnp.zeros_like(acc_ref)

    act = jnp.square(jnp.maximum(_mm(h_ref[...], wu_ref[...]), 0.0))
    acc_ref[...] += _mm(act.astype(BF16), wd_ref[...])

    @pl.when(j == pl.num_programs(1) - 1)
    def _():
        x_new = x_ref[...] + _rms(acc_ref[...], gpost_ref[...])
        xo_ref[...] = x_new
        if emit_next:
            ho_ref[...] = _rms(x_new, gnext_ref[...]).astype(ho_ref.dtype)


def ffn(h, x, w_up, w_down, g_post, g_next=None, tm=512, tf=512):
    m, d = x.shape
    f = w_up.shape[1]
    emit_next = g_next is not None
    row_d = pl.BlockSpec((tm, d), lambda i, j: (i, 0))
    vec_d = pl.BlockSpec((1, d), lambda i, j: (0, 0))
    in_specs = [row_d, row_d, pl.BlockSpec((d, tf), lambda i, j: (0, j)), pl.BlockSpec((tf, d), lambda i, j: (j, 0)),
                vec_d]
    args = [h, x, w_up, w_down, g_post.reshape(1, d)]
    out_specs = [row_d]
    out_shape = [jax.ShapeDtypeStruct((m, d), F32)]
    if emit_next:
        in_specs.append(vec_d)
        args.append(g_next.reshape(1, d))
        out_specs.append(row_d)
        out_shape.append(jax.ShapeDtypeStruct((m, d), BF16))
    return pl.pallas_call(
        functools.partial(_ffn_kernel, emit_next=emit_next),
        grid=(m // tm, f // tf),
        in_specs=in_specs,
        out_specs=out_specs,
        out_shape=out_shape,
        scratch_shapes=[pltpu.VMEM((tm, d), F32)],
        compiler_params=_params("parallel", "arbitrary"),
        name="ffn",
    )(*args)


def _pack_w_in(w_in_l):
    w = BRANCH_WIDTH
    gla0 = 3 * w
    ml0 = gla0 + 2 * N_HEADS * GLA_DK + w + GLA_GATE_RANK + w
    ca0 = ml0 + 4 * w + 2 * N_HEADS
    c = lambda a, b: w_in_l[:, a:b]
    g_low0 = gla0 + 2 * N_HEADS * GLA_DK + w
    gates = jnp.concatenate([c(g_low0, g_low0 + GLA_GATE_RANK), c(ml0 + 4 * w, ml0 + 4 * w + 2 * N_HEADS)], axis=1)
    gates = jnp.pad(gates, ((0, 0), (0, LANES - gates.shape[1])))
    w_mm = jnp.concatenate([c(0, 3 * w), c(gla0, gla0 + 2 * N_HEADS * GLA_DK + w),
                            c(ml0 + 2 * w, ml0 + 3 * w), c(ca0, ca0 + 3 * w)], axis=1)
    w_fx = jnp.concatenate([c(ml0, ml0 + 2 * w), c(g_low0 + GLA_GATE_RANK, g_low0 + GLA_GATE_RANK + w),
                            c(ml0 + 3 * w, ml0 + 4 * w), gates], axis=1)
    return w_mm, w_fx


def kernel(x, norm_mix_pre, norm_mix_post, w_in, diff_lambda, diff_norm, gla_w_gate_up, gla_b_gate, gla_norm,
           mlstm_conv_w, mlstm_conv_b, mlstm_b_i, mlstm_b_f, mlstm_norm, rel_bias, w_branch, w_gate, b_gate,
           w_out, norm_ffn_pre, norm_ffn_post, w_up, w_down):
    batch, seq, d = x.shape
    depth = w_in.shape[0]
    m = batch * seq
    xf = x.reshape(m, d)
    bias_tiles = band_bias(rel_bias)
    h = rmsnorm_cast(xf, norm_mix_pre[0])
    for l in range(depth):
        w_mm, w_fx = (a.astype(BF16) for a in _pack_w_in(w_in[l]))
        u_mm = matmul(h, w_mm, BF16, tm=min(1024, m), tn=MM_COLS // 3, name="in_proj_mm")
        u_fx = matmul(h, w_fx, F32, tm=512, tn=FX_COLS, name="in_proj_fx")
        branches = (
            diff_attention(u_mm, diff_lambda[l], diff_norm[l], l, batch, seq),
            gla(u_mm, u_fx, gla_w_gate_up[l], gla_b_gate[l], gla_norm[l], batch, seq),
            mlstm(u_mm, u_fx, mlstm_conv_w[l], mlstm_conv_b[l], mlstm_b_i[l], mlstm_b_f[l], mlstm_norm[l],
                  batch, seq),
            band_attention(u_mm, bias_tiles, l, batch, seq),
        )
        xf, h2 = merge(h, branches, xf, w_gate[l].astype(BF16), b_gate[l], w_branch[l].astype(BF16),
                       w_out[l].astype(BF16), norm_mix_post[l], norm_ffn_pre[l])
        g_next = norm_mix_pre[l + 1] if l + 1 < depth else None
        outs = ffn(h2, xf, w_up[l].astype(BF16), w_down[l].astype(BF16), norm_ffn_post[l], g_next)
        xf, h = outs if g_next is not None else (outs[0], None)
    return xf.reshape(batch, seq, d)
```

```python
import functools
import math

import jax
import jax.numpy as jnp
from jax import lax
from jax.experimental import pallas as pl
from jax.experimental.pallas import tpu as pltpu

F32 = jnp.float32
BF16 = jnp.bfloat16

D_MODEL = 2048
CHUNK = 64
BRANCH_WIDTH = 512
N_HEADS = 4
HEAD_DV = 128
DIFF_QK_DIM = 64
GLA_DK = 64
GLA_GATE_RANK = 16
GLA_GATE_NORM = 16.0
MLSTM_CONV = 4
CA_LEFT_CHUNKS = 8
REL_CLIP = 2 * CHUNK
D_FF = 4 * D_MODEL
EPS = 1e-6
NEG = -1e30

LANES = 128
VMEM_LIMIT = 56 * 1024 * 1024

MM_DIFF_Q, MM_DIFF_K, MM_DIFF_V = 0, 512, 1024
MM_GLA_Q, MM_GLA_K, MM_GLA_V = 1536, 1792, 2048
MM_ML_V = 2560
MM_CA_Q, MM_CA_K, MM_CA_V = 3072, 3584, 4096
MM_COLS = 4608
FX_ML_QK, FX_GLA_R, FX_ML_O, FX_GATES = 0, 1024, 1536, 2048
FX_COLS = 2176
GATE_GLA_LOW, GATE_ML_I, GATE_ML_F = 0, 16, 20

NT_DIMS = (((1,), (1,)), ((), ()))
TN_DIMS = (((0,), (0,)), ((), ()))


def _nt(a, b):
    return lax.dot_general(a, b, NT_DIMS, preferred_element_type=F32)


def _tn(a, b):
    return lax.dot_general(a, b, TN_DIMS, preferred_element_type=F32)


def _mm(a, b):
    return jnp.dot(a, b, preferred_element_type=F32)


def _mm_exact(a, b):
    return jnp.dot(a, b, preferred_element_type=F32, precision=lax.Precision.HIGHEST)


def _rms(x, gain):
    return x * lax.rsqrt(jnp.mean(x * x, axis=-1, keepdims=True) + EPS) * gain


def _log_sigmoid(x):
    return jnp.minimum(x, 0.0) - jnp.log1p(jnp.exp(-jnp.abs(x)))


def _sigmoid(x):
    return 1.0 / (1.0 + jnp.exp(-x))


def _params(*sem):
    return pltpu.CompilerParams(dimension_semantics=sem, vmem_limit_bytes=VMEM_LIMIT)


def _rmsnorm_cast_kernel(x_ref, g_ref, o_ref):
    o_ref[...] = _rms(x_ref[...], g_ref[...]).astype(o_ref.dtype)


def rmsnorm_cast(x, g, tm=512):
    m, d = x.shape
    return pl.pallas_call(
        _rmsnorm_cast_kernel,
        grid=(m // tm,),
        in_specs=[pl.BlockSpec((tm, d), lambda i: (i, 0)), pl.BlockSpec((1, d), lambda i: (0, 0))],
        out_specs=pl.BlockSpec((tm, d), lambda i: (i, 0)),
        out_shape=jax.ShapeDtypeStruct((m, d), BF16),
        compiler_params=_params("parallel"),
        name="rmsnorm_cast",
    )(x, g.reshape(1, d))


def _matmul_kernel(a_ref, b_ref, o_ref):
    o_ref[...] = _mm(a_ref[...], b_ref[...]).astype(o_ref.dtype)


def matmul(a, b, out_dtype, tm, tn, name):
    m, k = a.shape
    n = b.shape[1]
    return pl.pallas_call(
        _matmul_kernel,
        grid=(m // tm, n // tn),
        in_specs=[pl.BlockSpec((tm, k), lambda i, j: (i, 0)), pl.BlockSpec((k, tn), lambda i, j: (0, j))],
        out_specs=pl.BlockSpec((tm, tn), lambda i, j: (i, j)),
        out_shape=jax.ShapeDtypeStruct((m, n), out_dtype),
        compiler_params=_params("parallel", "arbitrary"),
        name=name,
    )(a, b)


def _diff_attn_kernel(slopes_ref, lam_ref, gain_ref, q_ref, k_ref, v_ref, o_ref,
                      vt_ref, kb_ref, diag_ref, m_ref, acc_ref, *, t, lam_init):
    h = pl.program_id(1)
    qi = pl.program_id(2)
    slope = slopes_ref[h]
    q0 = pl.multiple_of(qi * t, t)

    @pl.when(qi == 0)
    def _():
        vt_ref[0:HEAD_DV, :] = v_ref[...].astype(F32).T.astype(BF16)
        vt_ref[HEAD_DV:2 * HEAD_DV, :] = jnp.ones((HEAD_DV, v_ref.shape[0]), BF16)
        key = lax.broadcasted_iota(jnp.int32, (t, t), 0)
        qry = lax.broadcasted_iota(jnp.int32, (t, t), 1)
        past = slope * key.astype(F32)
        visible = key < (qry // CHUNK + 1) * CHUNK
        diag = jnp.where(visible, slope * (qry - jnp.abs(qry - key)).astype(F32), NEG)
        for mi in range(2):
            kb_ref[:, mi * t:(mi + 1) * t] = past
            diag_ref[:, mi * t:(mi + 1) * t] = diag

    lanes = lax.broadcasted_iota(jnp.int32, (1, 2 * DIFF_QK_DIM), 1)
    q = q_ref[...] * jnp.asarray(DIFF_QK_DIM ** -0.5, BF16)
    q_maps = jnp.concatenate([jnp.where(lanes < DIFF_QK_DIM, q, jnp.zeros_like(q)),
                              jnp.where(lanes >= DIFF_QK_DIM, q, jnp.zeros_like(q))], axis=0)
    m_ref[...] = jnp.full(m_ref.shape, NEG, F32)
    acc_ref[...] = jnp.zeros_like(acc_ref)

    def accumulate(z, shift, vt):
        m_prev = m_ref[...]
        m_new = jnp.maximum(m_prev, jnp.max(z, axis=0, keepdims=True) + shift)
        p = jnp.exp(z - (m_new - shift))
        acc_ref[...] = jnp.exp(m_prev - m_new) * acc_ref[...] + _mm(vt, p.astype(BF16))
        m_ref[...] = m_new

    def past_tile(j, _):
        k0 = pl.multiple_of(j * t, t)
        z = _nt(k_ref[pl.ds(k0, t), :], q_maps) + kb_ref[...]
        accumulate(z, slope * (k0 - q0).astype(F32), vt_ref[:, pl.ds(k0, t)])
        return 0

    lax.fori_loop(0, qi, past_tile, 0)
    z = _nt(k_ref[pl.ds(q0, t), :], q_maps) + diag_ref[...]
    accumulate(z, 0.0, vt_ref[:, pl.ds(q0, t)])

    lam = lam_ref[...]
    lam_full = (jnp.exp(jnp.sum(lam[0:1] * lam[1:2], axis=-1, keepdims=True))
                - jnp.exp(jnp.sum(lam[2:3] * lam[3:4], axis=-1, keepdims=True)) + lam_init)
    acc = acc_ref[...]
    ratio = acc[:HEAD_DV] / acc[HEAD_DV:]
    o = (ratio[:, :t] - lam_full * ratio[:, t:]).T
    o_ref[...] = (_rms(o, gain_ref[...]) * (1.0 - lam_init)).astype(o_ref.dtype)


def diff_attention(u_mm, lam, gain, layer_idx, batch, seq, tq=512):
    lam_init = 0.8 - 0.6 * math.exp(-0.3 * layer_idx)
    nq = seq // tq
    slopes = jnp.asarray([2.0 ** (-8.0 * (i + 1) / N_HEADS) for i in range(N_HEADS)], F32)
    cq, ck, cv = MM_DIFF_Q // LANES, MM_DIFF_K // LANES, MM_DIFF_V // LANES
    kern = functools.partial(_diff_attn_kernel, t=tq, lam_init=lam_init)
    return pl.pallas_call(
        kern,
        grid=(batch, N_HEADS, nq),
        in_specs=[
            pl.BlockSpec(memory_space=pltpu.SMEM),
            pl.BlockSpec((4, DIFF_QK_DIM), lambda b, h, i: (0, 0)),
            pl.BlockSpec((1, HEAD_DV), lambda b, h, i: (0, h)),
            pl.BlockSpec((tq, LANES), lambda b, h, i: (b * nq + i, cq + h)),
            pl.BlockSpec((seq, LANES), lambda b, h, i: (b, ck + h)),
            pl.BlockSpec((seq, LANES), lambda b, h, i: (b, cv + h)),
        ],
        out_specs=pl.BlockSpec((tq, HEAD_DV), lambda b, h, i: (b * nq + i, h)),
        out_shape=jax.ShapeDtypeStruct((batch * seq, BRANCH_WIDTH), BF16),
        scratch_shapes=[pltpu.VMEM((2 * HEAD_DV, seq), BF16),
                        pltpu.VMEM((tq, 2 * tq), F32),
                        pltpu.VMEM((tq, 2 * tq), F32),
                        pltpu.VMEM((1, 2 * tq), F32),
                        pltpu.VMEM((2 * HEAD_DV, 2 * tq), F32)],
        compiler_params=_params("parallel", "parallel", "arbitrary"),
        name="diff_attention",
    )(slopes, lam, gain.reshape(1, BRANCH_WIDTH), u_mm, u_mm, u_mm)


def _gla_kernel(q_ref, k_ref, v_ref, r_ref, g_ref, wgu_ref, bg_ref, gain_ref, o_ref, st_ref, *, tt):
    @pl.when(pl.program_id(1) == 0)
    def _():
        st_ref[...] = jnp.zeros_like(st_ref)

    row = lax.broadcasted_iota(jnp.int32, (CHUNK, CHUNK), 0)
    col = lax.broadcasted_iota(jnp.int32, (CHUNK, CHUNK), 1)
    causal = row >= col
    tril = causal.astype(F32)

    def chunk(c, _):
        r0 = pl.multiple_of(c * CHUNK, CHUNK)
        rows = pl.ds(r0, CHUNK)
        pre = _mm(g_ref[rows, :].astype(BF16), wgu_ref[...]) + bg_ref[...]
        log_a = _log_sigmoid(pre) * (1.0 / GLA_GATE_NORM)
        cum = _mm_exact(tril, log_a)
        total = cum[CHUNK - 1:CHUNK, :]
        qf = q_ref[rows, :].astype(F32) * (GLA_DK ** -0.5)
        kf = k_ref[rows, :].astype(F32)
        q_dec = (qf * jnp.exp(cum)).astype(BF16)
        k_inv = (kf * jnp.exp(-cum)).astype(BF16)
        k_tail = (kf * jnp.exp(total - cum)).astype(BF16)
        decay = jnp.exp(total)
        st = st_ref[...]
        for h in range(N_HEADS):
            ks = slice(h * GLA_DK, (h + 1) * GLA_DK)
            vs = slice(h * HEAD_DV, (h + 1) * HEAD_DV)
            vh = v_ref[rows, vs]
            att = jnp.where(causal, _nt(q_dec[:, ks], k_inv[:, ks]), 0.0)
            o = _mm(att.astype(BF16), vh) + _nt(q_dec[:, ks], st[:, ks].astype(BF16))
            st_ref[:, ks] = decay[:, ks] * st[:, ks] + _tn(vh, k_tail[:, ks])
            r = r_ref[rows, vs]
            y = _rms(o, gain_ref[:, vs]) * (r * _sigmoid(r))
            o_ref[rows, vs] = y.astype(o_ref.dtype)
        return 0

    lax.fori_loop(0, tt // CHUNK, chunk, 0)


def gla(u_mm, u_fx, w_gate_up, b_gate, gain, batch, seq, tt=1024):
    dkh = N_HEADS * GLA_DK
    tt = min(tt, seq)
    nt = seq // tt
    wgu = jnp.zeros((LANES, dkh), F32).at[GATE_GLA_LOW:GATE_GLA_LOW + GLA_GATE_RANK].set(w_gate_up).astype(BF16)
    kern = functools.partial(_gla_kernel, tt=tt)
    return pl.pallas_call(
        kern,
        grid=(batch, nt),
        in_specs=[
            pl.BlockSpec((tt, dkh), lambda b, t: (b * nt + t, MM_GLA_Q // dkh)),
            pl.BlockSpec((tt, dkh), lambda b, t: (b * nt + t, MM_GLA_K // dkh)),
            pl.BlockSpec((tt, BRANCH_WIDTH), lambda b, t: (b * nt + t, MM_GLA_V // BRANCH_WIDTH)),
            pl.BlockSpec((tt, BRANCH_WIDTH), lambda b, t: (b * nt + t, FX_GLA_R // BRANCH_WIDTH)),
            pl.BlockSpec((tt, LANES), lambda b, t: (b * nt + t, FX_GATES // LANES)),
            pl.BlockSpec((LANES, dkh), lambda b, t: (0, 0)),
            pl.BlockSpec((1, dkh), lambda b, t: (0, 0)),
            pl.BlockSpec((1, BRANCH_WIDTH), lambda b, t: (0, 0)),
        ],
        out_specs=pl.BlockSpec((tt, BRANCH_WIDTH), lambda b, t: (b * nt + t, 0)),
        out_shape=jax.ShapeDtypeStruct((batch * seq, BRANCH_WIDTH), BF16),
        scratch_shapes=[pltpu.VMEM((HEAD_DV, dkh), F32)],
        compiler_params=_params("parallel", "arbitrary"),
        name="gla",
    )(u_mm, u_mm, u_mm, u_fx, u_fx, wgu, b_gate.reshape(1, dkh), gain.reshape(1, BRANCH_WIDTH))


def _mlstm_kernel(qk_ref, v_ref, op_ref, g_ref, cw_ref, cb_ref, gb_ref, gain_ref, o_ref,
                  c_ref, n_ref, m_ref, halo_ref, *, tt):
    w = BRANCH_WIDTH
    halo = halo_ref.shape[0]

    @pl.when(pl.program_id(1) == 0)
    def _():
        c_ref[...] = jnp.zeros_like(c_ref)
        n_ref[...] = jnp.zeros_like(n_ref)
        m_ref[...] = jnp.zeros_like(m_ref)
        halo_ref[...] = jnp.zeros_like(halo_ref)

    row = lax.broadcasted_iota(jnp.int32, (CHUNK, CHUNK), 0)
    col = lax.broadcasted_iota(jnp.int32, (CHUNK, CHUNK), 1)
    causal = row >= col
    tril = causal.astype(F32)
    lane = lax.broadcasted_iota(jnp.int32, (1, LANES), 1)

    def chunk(c, _):
        r0 = pl.multiple_of(c * CHUNK, CHUNK)
        rows = pl.ds(r0, CHUNK)
        p0 = pl.multiple_of(jnp.maximum(r0 - halo, 0), halo)
        prev = jnp.where(c > 0, qk_ref[pl.ds(p0, halo), :], halo_ref[...])
        ext = jnp.concatenate([prev, qk_ref[rows, :]], axis=0)
        acc = cb_ref[...]
        for j in range(MLSTM_CONV):
            s0 = halo - (MLSTM_CONV - 1) + j
            acc = acc + cw_ref[j:j + 1, :] * ext[s0:s0 + CHUNK, :]
        qk = acc * _sigmoid(acc)
        gb = g_ref[rows, :] + gb_ref[...]
        log_f = _log_sigmoid(gb)
        bcum = _mm_exact(tril, log_f)
        x = jnp.where(lane < GATE_ML_F, gb, bcum)
        xt = x.T
        for h in range(N_HEADS):
            hs = slice(h * HEAD_DV, (h + 1) * HEAD_DV)
            li, lf = GATE_ML_I + h, GATE_ML_F + h
            q = qk[:, h * HEAD_DV:(h + 1) * HEAD_DV].astype(BF16)
            kf = qk[:, w + h * HEAD_DV:w + (h + 1) * HEAD_DV] * (HEAD_DV ** -0.5)
            k = kf.astype(BF16)
            v = v_ref[rows, hs]
            bcum_c = bcum[:, lf:lf + 1]
            logi_c = gb[:, li:li + 1]
            btot = bcum_c[CHUNK - 1:CHUNK, :]
            r_row = xt[li:li + 1, :] - xt[lf:lf + 1, :]
            dlog = jnp.where(causal, bcum_c + r_row, NEG)
            tail = btot - bcum_c + logi_c
            m_loc = jnp.max(tail, axis=0, keepdims=True)
            w_tail = jnp.exp(tail - m_loc)
            kw = kf * w_tail
            d_c = _tn(kw.astype(BF16), v)
            d_n = jnp.sum(kw, axis=0, keepdims=True)
            c_prev = c_ref[h]
            n_prev = n_ref[h]
            m_prev = m_ref[h]
            inter_log = bcum_c + m_prev
            m_t = jnp.maximum(jnp.max(dlog, axis=-1, keepdims=True), inter_log)
            a_mat = jnp.exp(dlog - m_t) * _nt(q, k)
            inter_w = jnp.exp(inter_log - m_t)
            num = _mm(a_mat.astype(BF16), v) + inter_w * _mm(q, c_prev.astype(BF16))
            qn = jnp.sum(q.astype(F32) * n_prev, axis=-1, keepdims=True)
            den = jnp.sum(a_mat, axis=-1, keepdims=True) + inter_w * qn
            hh = num / jnp.maximum(jnp.abs(den), jnp.exp(-m_t))
            y = _rms(_sigmoid(op_ref[rows, hs]) * hh, gain_ref[:, hs])
            o_ref[rows, hs] = y.astype(o_ref.dtype)
            m_new = jnp.maximum(btot + m_prev, m_loc)
            a = jnp.exp(btot + m_prev - m_new)
            bb = jnp.exp(m_loc - m_new)
            c_ref[h] = a * c_prev + bb * d_c
            n_ref[h] = a * n_prev + bb * d_n
            m_ref[h] = m_new
        return 0

    lax.fori_loop(0, tt // CHUNK, chunk, 0)
    halo_ref[...] = qk_ref[tt - halo:tt, :]


def mlstm(u_mm, u_fx, conv_w, conv_b, b_i, b_f, gain, batch, seq, tt=1024):
    w = BRANCH_WIDTH
    tt = min(tt, seq)
    nt = seq // tt
    gate_bias = (jnp.zeros((1, LANES), F32).at[0, GATE_ML_I:GATE_ML_I + N_HEADS].set(b_i)
                 .at[0, GATE_ML_F:GATE_ML_F + N_HEADS].set(b_f))
    kern = functools.partial(_mlstm_kernel, tt=tt)
    const = lambda b, t: (0, 0)
    return pl.pallas_call(
        kern,
        grid=(batch, nt),
        in_specs=[
            pl.BlockSpec((tt, 2 * w), lambda b, t: (b * nt + t, FX_ML_QK // (2 * w))),
            pl.BlockSpec((tt, w), lambda b, t: (b * nt + t, MM_ML_V // w)),
            pl.BlockSpec((tt, w), lambda b, t: (b * nt + t, FX_ML_O // w)),
            pl.BlockSpec((tt, LANES), lambda b, t: (b * nt + t, FX_GATES // LANES)),
            pl.BlockSpec((MLSTM_CONV, 2 * w), const),
            pl.BlockSpec((1, 2 * w), const),
            pl.BlockSpec((1, LANES), const),
            pl.BlockSpec((1, w), const),
        ],
        out_specs=pl.BlockSpec((tt, w), lambda b, t: (b * nt + t, 0)),
        out_shape=jax.ShapeDtypeStruct((batch * seq, w), BF16),
        scratch_shapes=[pltpu.VMEM((N_HEADS, HEAD_DV, HEAD_DV), F32),
                        pltpu.VMEM((N_HEADS, 1, HEAD_DV), F32),
                        pltpu.VMEM((N_HEADS, 1, 1), F32),
                        pltpu.VMEM((8, 2 * w), F32)],
        compiler_params=_params("parallel", "arbitrary"),
        name="mlstm",
    )(u_fx, u_mm, u_fx, u_fx, conv_w, conv_b.reshape(1, 2 * w), gate_bias, gain.reshape(1, w))


BAND_Q_CHUNKS = 4
BAND_TQ = BAND_Q_CHUNKS * CHUNK
BAND_TK = (CA_LEFT_CHUNKS + BAND_Q_CHUNKS) * CHUNK
BAND_PAD = CA_LEFT_CHUNKS * CHUNK


def _band_bias_kernel(table_ref, o_ref):
    l = pl.program_id(0)
    h = pl.program_id(1)
    row = lax.broadcasted_iota(jnp.int32, (BAND_TQ, BAND_TK), 0)
    col = lax.broadcasted_iota(jnp.int32, (BAND_TQ, BAND_TK), 1)
    idx = jnp.clip(row - col + BAND_PAD, -REL_CLIP, REL_CLIP) + REL_CLIP
    delta = row // CHUNK + CA_LEFT_CHUNKS - col // CHUNK
    in_band = (delta >= 0) & (delta <= CA_LEFT_CHUNKS)

    def body(r, acc):
        return jnp.where(idx == r, table_ref[l, h, r], acc)

    bias = lax.fori_loop(0, 2 * REL_CLIP + 1, body, jnp.zeros((BAND_TQ, BAND_TK), F32))
    o_ref[0, 0] = jnp.where(in_band, bias, NEG)


def band_bias(rel_bias):
    depth = rel_bias.shape[0]
    return pl.pallas_call(
        _band_bias_kernel,
        grid=(depth, N_HEADS),
        in_specs=[pl.BlockSpec(memory_space=pltpu.SMEM)],
        out_specs=pl.BlockSpec((1, 1, BAND_TQ, BAND_TK), lambda l, h: (l, h, 0, 0)),
        out_shape=jax.ShapeDtypeStruct((depth, N_HEADS, BAND_TQ, BAND_TK), F32),
        compiler_params=_params("parallel", "parallel"),
        name="band_bias",
    )(rel_bias)


def _band_attn_kernel(bias_ref, q_ref, k_ref, v_ref, o_ref, kp_ref, vp_ref, *, seq):
    kp_ref[0:BAND_PAD, :] = jnp.zeros((BAND_PAD, HEAD_DV), BF16)
    vp_ref[0:BAND_PAD, :] = jnp.zeros((BAND_PAD, HEAD_DV), BF16)
    kp_ref[BAND_PAD:BAND_PAD + seq, :] = k_ref[...]
    vp_ref[BAND_PAD:BAND_PAD + seq, :] = v_ref[...]
    col = lax.broadcasted_iota(jnp.int32, (BAND_TQ, BAND_TK), 1)
    scale = HEAD_DV ** -0.5

    def tile(t, _):
        r0 = pl.multiple_of(t * BAND_TQ, BAND_TQ)
        q = q_ref[pl.ds(r0, BAND_TQ), :]
        kb = kp_ref[pl.ds(r0, BAND_TK), :]
        vb = vp_ref[pl.ds(r0, BAND_TK), :]
        s = _nt(q, kb) * scale + bias_ref[0, 0]
        s = jnp.where(col + r0 >= BAND_PAD, s, NEG)
        p = jnp.exp(s - jnp.max(s, axis=-1, keepdims=True))
        o = _mm(p.astype(BF16), vb) / jnp.sum(p, axis=-1, keepdims=True)
        o_ref[pl.ds(r0, BAND_TQ), :] = o.astype(o_ref.dtype)
        return 0

    lax.fori_loop(0, seq // BAND_TQ, tile, 0)


def band_attention(u_mm, bias, layer, batch, seq):
    cq, ck, cv = MM_CA_Q // LANES, MM_CA_K // LANES, MM_CA_V // LANES
    kern = functools.partial(_band_attn_kernel, seq=seq)
    return pl.pallas_call(
        kern,
        grid=(batch, N_HEADS),
        in_specs=[
            pl.BlockSpec((1, 1, BAND_TQ, BAND_TK), lambda b, h: (layer, h, 0, 0)),
            pl.BlockSpec((seq, LANES), lambda b, h: (b, cq + h)),
            pl.BlockSpec((seq, LANES), lambda b, h: (b, ck + h)),
            pl.BlockSpec((seq, LANES), lambda b, h: (b, cv + h)),
        ],
        out_specs=pl.BlockSpec((seq, HEAD_DV), lambda b, h: (b, h)),
        out_shape=jax.ShapeDtypeStruct((batch * seq, BRANCH_WIDTH), BF16),
        scratch_shapes=[pltpu.VMEM((BAND_PAD + seq, HEAD_DV), BF16),
                        pltpu.VMEM((BAND_PAD + seq, HEAD_DV), BF16)],
        compiler_params=_params("parallel", "parallel"),
        name="band_attention",
    )(bias, u_mm, u_mm, u_mm)


def _gated_merge_kernel(h_ref, ba_ref, bb_ref, bc_ref, bd_ref, wg_ref, bg_ref, wb_ref, o_ref):
    h = h_ref[...]
    merged = None
    for bi, br_ref in enumerate((ba_ref, bb_ref, bc_ref, bd_ref)):
        gate = _sigmoid(_mm(h, wg_ref[bi]) + bg_ref[bi])
        term = gate * _mm(br_ref[...], wb_ref[bi])
        merged = term if merged is None else merged + term
    o_ref[...] = merged.astype(o_ref.dtype)


def gated_merge(h, branches, w_gate, b_gate, w_branch, tm=1024, tn=512):
    m, d = h.shape
    w = BRANCH_WIDTH
    tm = min(tm, m)
    row_d = pl.BlockSpec((tm, d), lambda i, j: (i, 0))
    row_w = pl.BlockSpec((tm, w), lambda i, j: (i, 0))
    return pl.pallas_call(
        _gated_merge_kernel,
        grid=(m // tm, d // tn),
        in_specs=[
            row_d, row_w, row_w, row_w, row_w,
            pl.BlockSpec((4, d, tn), lambda i, j: (0, 0, j)),
            pl.BlockSpec((4, 1, tn), lambda i, j: (0, 0, j)),
            pl.BlockSpec((4, w, tn), lambda i, j: (0, 0, j)),
        ],
        out_specs=pl.BlockSpec((tm, tn), lambda i, j: (i, j)),
        out_shape=jax.ShapeDtypeStruct((m, d), BF16),
        compiler_params=_params("parallel", "arbitrary"),
        name="gated_merge",
    )(h, *branches, w_gate, b_gate.reshape(4, 1, d), w_branch)


def _out_proj_kernel(a_ref, x_ref, wo_ref, gpost_ref, gnext_ref, xo_ref, ho_ref):
    x_new = x_ref[...] + _rms(_mm(a_ref[...], wo_ref[...]), gpost_ref[...])
    xo_ref[...] = x_new
    ho_ref[...] = _rms(x_new, gnext_ref[...]).astype(ho_ref.dtype)


def out_proj(a, x, w_out, g_post, g_next, tm=512):
    m, d = x.shape
    row_d = pl.BlockSpec((tm, d), lambda i: (i, 0))
    vec_d = pl.BlockSpec((1, d), lambda i: (0, 0))
    return pl.pallas_call(
        _out_proj_kernel,
        grid=(m // tm,),
        in_specs=[row_d, row_d, pl.BlockSpec((d, d), lambda i: (0, 0)), vec_d, vec_d],
        out_specs=[row_d, row_d],
        out_shape=[jax.ShapeDtypeStruct((m, d), F32), jax.ShapeDtypeStruct((m, d), BF16)],
        compiler_params=_params("parallel"),
        name="out_proj",
    )(a, x, w_out, g_post.reshape(1, d), g_next.reshape(1, d))


def _ffn_kernel(h_ref, x_ref, wu_ref, wd_ref, gpost_ref, *rest, emit_next):
    if emit_next:
        gnext_ref, xo_ref, ho_ref = rest
    else:
        (xo_ref,) = rest
    j = pl.program_id(1)

    @pl.when(j == 0)
    def _():
        xo_ref[...] = jnp.zeros_like(xo_ref)

    act = jnp.square(jnp.maximum(_mm(h_ref[...], wu_ref[...]), 0.0))
    xo_ref[...] += _mm(act.astype(BF16), wd_ref[...])

    @pl.when(j == pl.num_programs(1) - 1)
    def _():
        x_new = x_ref[...] + _rms(xo_ref[...], gpost_ref[...])
        xo_ref[...] = x_new
        if emit_next:
            ho_ref[...] = _rms(x_new, gnext_ref[...]).astype(ho_ref.dtype)


def ffn(h, x, w_up, w_down, g_post, g_next=None, tm=1024, tf=512):
    m, d = x.shape
    f = w_up.shape[1]
    tm = min(tm, m)
    emit_next = g_next is not None
    row_d = pl.BlockSpec((tm, d), lambda i, j: (i, 0))
    vec_d = pl.BlockSpec((1, d), lambda i, j: (0, 0))
    x_spec = pl.BlockSpec((tm, d), lambda i, j: (i, 0), pipeline_mode=pl.Buffered(1))
    in_specs = [row_d, x_spec, pl.BlockSpec((d, tf), lambda i, j: (0, j)), pl.BlockSpec((tf, d), lambda i, j: (j, 0)),
                vec_d]
    args = [h, x, w_up, w_down, g_post.reshape(1, d)]
    out_specs = [row_d]
    out_shape = [jax.ShapeDtypeStruct((m, d), F32)]
    if emit_next:
        in_specs.append(vec_d)
        args.append(g_next.reshape(1, d))
        out_specs.append(row_d)
        out_shape.append(jax.ShapeDtypeStruct((m, d), BF16))
    return pl.pallas_call(
        functools.partial(_ffn_kernel, emit_next=emit_next),
        grid=(m // tm, f // tf),
        in_specs=in_specs,
        out_specs=out_specs,
        out_shape=out_shape,
        compiler_params=_params("parallel", "arbitrary"),
        name="ffn",
    )(*args)


def _pack_w_in(w_in_l):
    w = BRANCH_WIDTH
    gla0 = 3 * w
    ml0 = gla0 + 2 * N_HEADS * GLA_DK + w + GLA_GATE_RANK + w
    ca0 = ml0 + 4 * w + 2 * N_HEADS
    c = lambda a, b: w_in_l[:, a:b]
    g_low0 = gla0 + 2 * N_HEADS * GLA_DK + w
    gates = jnp.concatenate([c(g_low0, g_low0 + GLA_GATE_RANK), c(ml0 + 4 * w, ml0 + 4 * w + 2 * N_HEADS)], axis=1)
    gates = jnp.pad(gates, ((0, 0), (0, LANES - gates.shape[1])))
    w_mm = jnp.concatenate([c(0, 3 * w), c(gla0, gla0 + 2 * N_HEADS * GLA_DK + w),
                            c(ml0 + 2 * w, ml0 + 3 * w), c(ca0, ca0 + 3 * w)], axis=1)
    w_fx = jnp.concatenate([c(ml0, ml0 + 2 * w), c(g_low0 + GLA_GATE_RANK, g_low0 + GLA_GATE_RANK + w),
                            c(ml0 + 3 * w, ml0 + 4 * w), gates], axis=1)
    return w_mm, w_fx


def kernel(x, norm_mix_pre, norm_mix_post, w_in, diff_lambda, diff_norm, gla_w_gate_up, gla_b_gate, gla_norm,
           mlstm_conv_w, mlstm_conv_b, mlstm_b_i, mlstm_b_f, mlstm_norm, rel_bias, w_branch, w_gate, b_gate,
           w_out, norm_ffn_pre, norm_ffn_post, w_up, w_down):
    batch, seq, d = x.shape
    depth = w_in.shape[0]
    m = batch * seq
    xf = x.reshape(m, d)
    bias_tiles = band_bias(rel_bias)
    h = rmsnorm_cast(xf, norm_mix_pre[0])
    for l in range(depth):
        w_mm, w_fx = (a.astype(BF16) for a in _pack_w_in(w_in[l]))
        u_mm = matmul(h, w_mm, BF16, tm=min(1024, m), tn=MM_COLS // 3, name="in_proj_mm")
        u_fx = matmul(h, w_fx, F32, tm=512, tn=FX_COLS, name="in_proj_fx")
        branches = (
            diff_attention(u_mm, diff_lambda[l], diff_norm[l], l, batch, seq),
            gla(u_mm, u_fx, gla_w_gate_up[l], gla_b_gate[l], gla_norm[l], batch, seq),
            mlstm(u_mm, u_fx, mlstm_conv_w[l], mlstm_conv_b[l], mlstm_b_i[l], mlstm_b_f[l], mlstm_norm[l],
                  batch, seq),
            band_attention(u_mm, bias_tiles, l, batch, seq),
        )
        merged = gated_merge(h, branches, w_gate[l].astype(BF16), b_gate[l], w_branch[l].astype(BF16))
        xf, h2 = out_proj(merged, xf, w_out[l].astype(BF16), norm_mix_post[l], norm_ffn_pre[l])
        g_next = norm_mix_pre[l + 1] if l + 1 < depth else None
        outs = ffn(h2, xf, w_up[l].astype(BF16), w_down[l].astype(BF16), norm_ffn_post[l], g_next)
        xf, h = outs if g_next is not None else (outs[0], None)
    return xf.reshape(batch, seq, d)
```

```python
import functools
import math

import jax
import jax.numpy as jnp
from jax import lax
from jax.experimental import pallas as pl
from jax.experimental.pallas import tpu as pltpu

F32 = jnp.float32
BF16 = jnp.bfloat16

D_MODEL = 2048
CHUNK = 64
BRANCH_WIDTH = 512
N_HEADS = 4
HEAD_DV = 128
DIFF_QK_DIM = 64
GLA_DK = 64
GLA_GATE_RANK = 16
GLA_GATE_NORM = 16.0
MLSTM_CONV = 4
CA_LEFT_CHUNKS = 8
REL_CLIP = 2 * CHUNK
D_FF = 4 * D_MODEL
EPS = 1e-6
NEG = -1e30

LANES = 128
VMEM_LIMIT = 56 * 1024 * 1024

MM_DIFF_Q, MM_DIFF_K, MM_DIFF_V = 0, 512, 1024
MM_GLA_Q, MM_GLA_K, MM_GLA_V = 1536, 1792, 2048
MM_ML_V = 2560
MM_CA_Q, MM_CA_K, MM_CA_V = 3072, 3584, 4096
MM_COLS = 4608
FX_ML_QK, FX_GLA_R, FX_ML_O, FX_GATES = 0, 1024, 1536, 2048
FX_COLS = 2176
GATE_GLA_LOW, GATE_ML_I, GATE_ML_F = 0, 16, 20

NT_DIMS = (((1,), (1,)), ((), ()))
TN_DIMS = (((0,), (0,)), ((), ()))


def _nt(a, b):
    return lax.dot_general(a, b, NT_DIMS, preferred_element_type=F32)


def _tn(a, b):
    return lax.dot_general(a, b, TN_DIMS, preferred_element_type=F32)


def _mm(a, b):
    return jnp.dot(a, b, preferred_element_type=F32)


def _tri_cumsum(tril, x):
    hi = x.astype(BF16)
    rest = x - hi.astype(F32)
    mid = rest.astype(BF16)
    lo = (rest - mid.astype(F32)).astype(BF16)
    n = x.shape[1]
    parts = _mm(tril, jnp.concatenate([hi, mid, lo], axis=1))
    return parts[:, :n] + parts[:, n:2 * n] + parts[:, 2 * n:]


def _rms(x, gain):
    return x * lax.rsqrt(jnp.mean(x * x, axis=-1, keepdims=True) + EPS) * gain


def _log_sigmoid(x):
    return jnp.minimum(x, 0.0) - jnp.log1p(jnp.exp(-jnp.abs(x)))


def _sigmoid(x):
    return 1.0 / (1.0 + jnp.exp(-x))


def _params(*sem):
    return pltpu.CompilerParams(dimension_semantics=sem, vmem_limit_bytes=VMEM_LIMIT)


def _rmsnorm_cast_kernel(x_ref, g_ref, o_ref):
    o_ref[...] = _rms(x_ref[...], g_ref[...]).astype(o_ref.dtype)


def rmsnorm_cast(x, g, tm=512):
    m, d = x.shape
    return pl.pallas_call(
        _rmsnorm_cast_kernel,
        grid=(m // tm,),
        in_specs=[pl.BlockSpec((tm, d), lambda i: (i, 0)), pl.BlockSpec((1, d), lambda i: (0, 0))],
        out_specs=pl.BlockSpec((tm, d), lambda i: (i, 0)),
        out_shape=jax.ShapeDtypeStruct((m, d), BF16),
        compiler_params=_params("parallel"),
        name="rmsnorm_cast",
    )(x, g.reshape(1, d))


def _matmul_kernel(a_ref, b_ref, o_ref):
    o_ref[...] = _mm(a_ref[...], b_ref[...]).astype(o_ref.dtype)


def matmul(a, b, out_dtype, tm, tn, name):
    m, k = a.shape
    n = b.shape[1]
    return pl.pallas_call(
        _matmul_kernel,
        grid=(m // tm, n // tn),
        in_specs=[pl.BlockSpec((tm, k), lambda i, j: (i, 0)), pl.BlockSpec((k, tn), lambda i, j: (0, j))],
        out_specs=pl.BlockSpec((tm, tn), lambda i, j: (i, j)),
        out_shape=jax.ShapeDtypeStruct((m, n), out_dtype),
        compiler_params=_params("parallel", "arbitrary"),
        name=name,
    )(a, b)


def _diff_attn_kernel(slopes_ref, lam_ref, gain_ref, q_ref, k_ref, v_ref, o_ref,
                      vt_ref, kb_ref, diag_ref, m_ref, acc_ref, *, t, lam_init):
    h = pl.program_id(1)
    qi = pl.program_id(2)
    slope = slopes_ref[h]
    q0 = pl.multiple_of(qi * t, t)

    @pl.when(qi == 0)
    def _():
        vt_ref[0:HEAD_DV, :] = v_ref[...].astype(F32).T.astype(BF16)
        vt_ref[HEAD_DV:2 * HEAD_DV, :] = jnp.ones((HEAD_DV, v_ref.shape[0]), BF16)
        key = lax.broadcasted_iota(jnp.int32, (t, t), 0)
        qry = lax.broadcasted_iota(jnp.int32, (t, t), 1)
        past = slope * key.astype(F32)
        visible = key < (qry // CHUNK + 1) * CHUNK
        diag = jnp.where(visible, slope * (qry - jnp.abs(qry - key)).astype(F32), NEG)
        for mi in range(2):
            kb_ref[:, mi * t:(mi + 1) * t] = past
            diag_ref[:, mi * t:(mi + 1) * t] = diag

    lanes = lax.broadcasted_iota(jnp.int32, (1, 2 * DIFF_QK_DIM), 1)
    q = q_ref[...] * jnp.asarray(DIFF_QK_DIM ** -0.5, BF16)
    q_maps = jnp.concatenate([jnp.where(lanes < DIFF_QK_DIM, q, jnp.zeros_like(q)),
                              jnp.where(lanes >= DIFF_QK_DIM, q, jnp.zeros_like(q))], axis=0)
    m_ref[...] = jnp.full(m_ref.shape, NEG, F32)
    acc_ref[...] = jnp.zeros_like(acc_ref)

    def accumulate(z, shift, vt):
        m_prev = m_ref[...]
        m_new = jnp.maximum(m_prev, jnp.max(z, axis=0, keepdims=True) + shift)
        p = jnp.exp(z - (m_new - shift))
        acc_ref[...] = jnp.exp(m_prev - m_new) * acc_ref[...] + _mm(vt, p.astype(BF16))
        m_ref[...] = m_new

    def past_tile(j, _):
        k0 = pl.multiple_of(j * t, t)
        z = _nt(k_ref[pl.ds(k0, t), :], q_maps) + kb_ref[...]
        accumulate(z, slope * (k0 - q0).astype(F32), vt_ref[:, pl.ds(k0, t)])
        return 0

    def two_past_tiles(jj, _):
        past_tile(2 * jj, 0)
        past_tile(2 * jj + 1, 0)
        return 0

    lax.fori_loop(0, qi // 2, two_past_tiles, 0)

    @pl.when(qi % 2 == 1)
    def _():
        past_tile(qi - 1, 0)

    z = _nt(k_ref[pl.ds(q0, t), :], q_maps) + diag_ref[...]
    accumulate(z, 0.0, vt_ref[:, pl.ds(q0, t)])

    lam = lam_ref[...]
    lam_full = (jnp.exp(jnp.sum(lam[0:1] * lam[1:2], axis=-1, keepdims=True))
                - jnp.exp(jnp.sum(lam[2:3] * lam[3:4], axis=-1, keepdims=True)) + lam_init)
    acc = acc_ref[...]
    ratio = acc[:HEAD_DV] / acc[HEAD_DV:]
    o = (ratio[:, :t] - lam_full * ratio[:, t:]).T
    o_ref[...] = (_rms(o, gain_ref[...]) * (1.0 - lam_init)).astype(o_ref.dtype)


def diff_attention(u_mm, lam, gain, layer_idx, batch, seq, tq=512):
    lam_init = 0.8 - 0.6 * math.exp(-0.3 * layer_idx)
    nq = seq // tq
    slopes = jnp.asarray([2.0 ** (-8.0 * (i + 1) / N_HEADS) for i in range(N_HEADS)], F32)
    cq, ck, cv = MM_DIFF_Q // LANES, MM_DIFF_K // LANES, MM_DIFF_V // LANES
    kern = functools.partial(_diff_attn_kernel, t=tq, lam_init=lam_init)
    return pl.pallas_call(
        kern,
        grid=(batch, N_HEADS, nq),
        in_specs=[
            pl.BlockSpec(memory_space=pltpu.SMEM),
            pl.BlockSpec((4, DIFF_QK_DIM), lambda b, h, i: (0, 0)),
            pl.BlockSpec((1, HEAD_DV), lambda b, h, i: (0, h)),
            pl.BlockSpec((tq, LANES), lambda b, h, i: (b * nq + i, cq + h)),
            pl.BlockSpec((seq, LANES), lambda b, h, i: (b, ck + h)),
            pl.BlockSpec((seq, LANES), lambda b, h, i: (b, cv + h)),
        ],
        out_specs=pl.BlockSpec((tq, HEAD_DV), lambda b, h, i: (b * nq + i, h)),
        out_shape=jax.ShapeDtypeStruct((batch * seq, BRANCH_WIDTH), BF16),
        scratch_shapes=[pltpu.VMEM((2 * HEAD_DV, seq), BF16),
                        pltpu.VMEM((tq, 2 * tq), F32),
                        pltpu.VMEM((tq, 2 * tq), F32),
                        pltpu.VMEM((1, 2 * tq), F32),
                        pltpu.VMEM((2 * HEAD_DV, 2 * tq), F32)],
        compiler_params=_params("parallel", "parallel", "arbitrary"),
        name="diff_attention",
    )(slopes, lam, gain.reshape(1, BRANCH_WIDTH), u_mm, u_mm, u_mm)


def _recurrent_mixers_kernel(
        g_ref,
        gq_ref, gk_ref, gv_ref, gr_ref, wgu_ref, bg_ref, ggain_ref,
        qk_ref, v_ref, op_ref, cw_ref, cb_ref, gb_ref, gain_ref,
        go_ref, o_ref,
        st_ref, c_ref, n_ref, m_ref, halo_ref, *, tt):
    w = BRANCH_WIDTH
    halo = halo_ref.shape[0]

    @pl.when(pl.program_id(1) == 0)
    def _():
        st_ref[...] = jnp.zeros_like(st_ref)
        c_ref[...] = jnp.zeros_like(c_ref)
        n_ref[...] = jnp.zeros_like(n_ref)
        m_ref[...] = jnp.zeros_like(m_ref)
        halo_ref[...] = jnp.zeros_like(halo_ref)

    dkh = N_HEADS * GLA_DK
    hc = N_HEADS * CHUNK
    iota = lambda shape, dim: lax.broadcasted_iota(jnp.int32, shape, dim)
    tril = (iota((CHUNK, CHUNK), 0) >= iota((CHUNK, CHUNK), 1)).astype(BF16)
    lane = iota((1, LANES), 1)
    head_of_dk_lane = iota((1, dkh), 1) // GLA_DK
    head_of_dv_lane = iota((1, w), 1) // HEAD_DV
    gla_causal = iota((CHUNK, hc), 0) >= iota((CHUNK, hc), 1) % CHUNK
    gla_state_block = iota((w, dkh), 0) // HEAD_DV == iota((w, dkh), 1) // GLA_DK
    r_i, c_i = iota((hc, hc), 0), iota((hc, hc), 1)
    ml_visible = (r_i // CHUNK == c_i // CHUNK) & (r_i % CHUNK >= c_i % CHUNK)

    def per_head_rows(a, width):
        return jnp.concatenate([a[:, h * width:(h + 1) * width] for h in range(N_HEADS)], axis=0)

    def head_blocks(a, head_of_lane):
        return jnp.concatenate([jnp.where(head_of_lane == h, a, jnp.zeros_like(a)) for h in range(N_HEADS)], axis=0)

    def gla_chunk(c):
        r0 = pl.multiple_of(c * CHUNK, CHUNK)
        rows = pl.ds(r0, CHUNK)
        pre = _mm(g_ref[rows, :].astype(BF16), wgu_ref[...]) + bg_ref[...]
        log_a = _log_sigmoid(pre) * (1.0 / GLA_GATE_NORM)
        cum = _tri_cumsum(tril, log_a)
        total = cum[CHUNK - 1:CHUNK, :]
        qf = gq_ref[rows, :].astype(F32) * (GLA_DK ** -0.5)
        kf = gk_ref[rows, :].astype(F32)
        q_dec = (qf * jnp.exp(cum)).astype(BF16)
        k_inv = (kf * jnp.exp(-cum)).astype(BF16)
        k_tail = (kf * jnp.exp(total - cum)).astype(BF16)
        v = gv_ref[rows, :]
        att = jnp.where(gla_causal, _nt(q_dec, head_blocks(k_inv, head_of_dk_lane)), 0.0)
        st = st_ref[...]
        o = _mm(att.astype(BF16), head_blocks(v, head_of_dv_lane)) + _nt(q_dec, st.astype(BF16))
        st_ref[...] = jnp.exp(total) * st + jnp.where(gla_state_block, _tn(v, k_tail), 0.0)
        for h in range(N_HEADS):
            vs = slice(h * HEAD_DV, (h + 1) * HEAD_DV)
            r = gr_ref[rows, vs]
            y = _rms(o[:, vs], ggain_ref[:, vs]) * (r * _sigmoid(r))
            go_ref[rows, vs] = y.astype(go_ref.dtype)

    def mlstm_chunk(c):
        r0 = pl.multiple_of(c * CHUNK, CHUNK)
        rows = pl.ds(r0, CHUNK)
        p0 = pl.multiple_of(jnp.maximum(r0 - halo, 0), halo)
        prev = jnp.where(c > 0, qk_ref[pl.ds(p0, halo), :], halo_ref[...])
        ext = jnp.concatenate([prev, qk_ref[rows, :]], axis=0)
        acc = cb_ref[...]
        for j in range(MLSTM_CONV):
            s0 = halo - (MLSTM_CONV - 1) + j
            acc = acc + cw_ref[j:j + 1, :] * ext[s0:s0 + CHUNK, :]
        qk = acc * _sigmoid(acc)
        gb = g_ref[rows, :] + gb_ref[...]
        log_f = _log_sigmoid(gb)
        bcum = _tri_cumsum(tril, log_f)
        x = jnp.where(lane < GATE_ML_F, gb, bcum)
        xt = x.T
        heads = range(N_HEADS)
        rep = lambda per_head: jnp.concatenate(
            [jnp.broadcast_to(a, (CHUNK, a.shape[-1])) for a in per_head], axis=0)
        bcum_c = jnp.concatenate([bcum[:, GATE_ML_F + h:GATE_ML_F + h + 1] for h in heads], axis=0)
        logi_c = jnp.concatenate([gb[:, GATE_ML_I + h:GATE_ML_I + h + 1] for h in heads], axis=0)
        r_row = jnp.concatenate([xt[GATE_ML_I + h:GATE_ML_I + h + 1, :] - xt[GATE_ML_F + h:GATE_ML_F + h + 1, :]
                                 for h in heads], axis=1)
        btot = [bcum[CHUNK - 1:CHUNK, GATE_ML_F + h:GATE_ML_F + h + 1] for h in heads]
        m_prev = [m_ref[h] for h in heads]
        tail = rep(btot) - bcum_c + logi_c
        m_loc = [jnp.max(tail[h * CHUNK:(h + 1) * CHUNK], axis=0, keepdims=True) for h in heads]
        w_tail = jnp.exp(tail - rep(m_loc))
        q_lanes = qk[:, :w].astype(BF16)
        q = per_head_rows(q_lanes, HEAD_DV)
        kf = per_head_rows(qk[:, w:], HEAD_DV) * (HEAD_DV ** -0.5)
        v = per_head_rows(v_ref[rows, :], HEAD_DV)
        kw = kf * w_tail
        dlog = jnp.where(ml_visible, bcum_c + r_row, NEG)
        inter_log = bcum_c + rep(m_prev)
        m_t = jnp.maximum(jnp.max(dlog, axis=-1, keepdims=True), inter_log)
        a_mat = jnp.exp(dlog - m_t) * _nt(q, kf.astype(BF16))
        inter_w = jnp.exp(inter_log - m_t)
        c_prev = c_ref[...]
        n_prev = [n_ref[h] for h in heads]
        num = _mm(a_mat.astype(BF16), v) + inter_w * _mm(head_blocks(q_lanes, head_of_dv_lane), c_prev.astype(BF16))
        qn = jnp.sum(q.astype(F32) * rep(n_prev), axis=-1, keepdims=True)
        den = jnp.sum(a_mat, axis=-1, keepdims=True) + inter_w * qn
        hh = num / jnp.maximum(jnp.abs(den), jnp.exp(-m_t))
        for h in heads:
            hs = slice(h * HEAD_DV, (h + 1) * HEAD_DV)
            y = _rms(_sigmoid(op_ref[rows, hs]) * hh[h * CHUNK:(h + 1) * CHUNK], gain_ref[:, hs])
            o_ref[rows, hs] = y.astype(o_ref.dtype)
        kw_lanes = jnp.concatenate([kw[h * CHUNK:(h + 1) * CHUNK] for h in heads], axis=1)
        d_c = _tn(head_blocks(kw_lanes.astype(BF16), head_of_dv_lane), v)
        rep_d = lambda per_head: jnp.concatenate(
            [jnp.broadcast_to(a, (HEAD_DV, 1)) for a in per_head], axis=0)
        m_new = [jnp.maximum(btot[h] + m_prev[h], m_loc[h]) for h in heads]
        a = [jnp.exp(btot[h] + m_prev[h] - m_new[h]) for h in heads]
        bb = [jnp.exp(m_loc[h] - m_new[h]) for h in heads]
        c_ref[...] = rep_d(a) * c_prev + rep_d(bb) * d_c
        for h in heads:
            d_n = jnp.sum(kw[h * CHUNK:(h + 1) * CHUNK], axis=0, keepdims=True)
            n_ref[h] = a[h] * n_prev[h] + bb[h] * d_n
            m_ref[h] = m_new[h]

    def chunk(c, _):
        gla_chunk(c)
        mlstm_chunk(c)
        return 0

    lax.fori_loop(0, tt // CHUNK, chunk, 0, unroll=4)
    halo_ref[...] = qk_ref[tt - halo:tt, :]


def recurrent_mixers(u_mm, u_fx, gla_w_gate_up, gla_b_gate, gla_gain, conv_w, conv_b, b_i, b_f, ml_gain,
                     batch, seq, tt=1024):
    w = BRANCH_WIDTH
    dkh = N_HEADS * GLA_DK
    tt = min(tt, seq)
    nt = seq // tt
    wgu = (jnp.zeros((LANES, dkh), F32).at[GATE_GLA_LOW:GATE_GLA_LOW + GLA_GATE_RANK].set(gla_w_gate_up)
           .astype(BF16))
    gate_bias = (jnp.zeros((1, LANES), F32).at[0, GATE_ML_I:GATE_ML_I + N_HEADS].set(b_i)
                 .at[0, GATE_ML_F:GATE_ML_F + N_HEADS].set(b_f))
    const = lambda b, t: (0, 0)
    rows = lambda width, col0: pl.BlockSpec((tt, width), lambda b, t: (b * nt + t, col0 // width))
    out = jax.ShapeDtypeStruct((batch * seq, w), BF16)
    return pl.pallas_call(
        functools.partial(_recurrent_mixers_kernel, tt=tt),
        grid=(batch, nt),
        in_specs=[
            rows(LANES, FX_GATES),
            rows(dkh, MM_GLA_Q), rows(dkh, MM_GLA_K), rows(w, MM_GLA_V), rows(w, FX_GLA_R),
            pl.BlockSpec((LANES, dkh), const), pl.BlockSpec((1, dkh), const), pl.BlockSpec((1, w), const),
            rows(2 * w, FX_ML_QK), rows(w, MM_ML_V), rows(w, FX_ML_O),
            pl.BlockSpec((MLSTM_CONV, 2 * w), const), pl.BlockSpec((1, 2 * w), const),
            pl.BlockSpec((1, LANES), const), pl.BlockSpec((1, w), const),
        ],
        out_specs=[rows(w, 0), rows(w, 0)],
        out_shape=[out, out],
        scratch_shapes=[pltpu.VMEM((w, dkh), F32),
                        pltpu.VMEM((N_HEADS * HEAD_DV, HEAD_DV), F32),
                        pltpu.VMEM((N_HEADS, 1, HEAD_DV), F32),
                        pltpu.VMEM((N_HEADS, 1, 1), F32),
                        pltpu.VMEM((8, 2 * w), F32)],
        compiler_params=_params("parallel", "arbitrary"),
        name="recurrent_mixers",
    )(u_fx,
      u_mm, u_mm, u_mm, u_fx, wgu, gla_b_gate.reshape(1, dkh), gla_gain.reshape(1, w),
      u_fx, u_mm, u_fx, conv_w, conv_b.reshape(1, 2 * w), gate_bias, ml_gain.reshape(1, w))


BAND_Q_CHUNKS = 4
BAND_TQ = BAND_Q_CHUNKS * CHUNK
BAND_TK = (CA_LEFT_CHUNKS + BAND_Q_CHUNKS) * CHUNK
BAND_PAD = CA_LEFT_CHUNKS * CHUNK


def _band_bias_kernel(table_ref, o_ref):
    l = pl.program_id(0)
    h = pl.program_id(1)
    row = lax.broadcasted_iota(jnp.int32, (BAND_TQ, BAND_TK), 0)
    col = lax.broadcasted_iota(jnp.int32, (BAND_TQ, BAND_TK), 1)
    idx = jnp.clip(row - col + BAND_PAD, -REL_CLIP, REL_CLIP) + REL_CLIP
    delta = row // CHUNK + CA_LEFT_CHUNKS - col // CHUNK
    in_band = (delta >= 0) & (delta <= CA_LEFT_CHUNKS)

    def body(r, acc):
        return jnp.where(idx == r, table_ref[l, h, r], acc)

    bias = lax.fori_loop(0, 2 * REL_CLIP + 1, body, jnp.zeros((BAND_TQ, BAND_TK), F32))
    o_ref[0, 0] = jnp.where(in_band, bias, NEG)


def band_bias(rel_bias):
    depth = rel_bias.shape[0]
    return pl.pallas_call(
        _band_bias_kernel,
        grid=(depth, N_HEADS),
        in_specs=[pl.BlockSpec(memory_space=pltpu.SMEM)],
        out_specs=pl.BlockSpec((1, 1, BAND_TQ, BAND_TK), lambda l, h: (l, h, 0, 0)),
        out_shape=jax.ShapeDtypeStruct((depth, N_HEADS, BAND_TQ, BAND_TK), F32),
        compiler_params=_params("parallel", "parallel"),
        name="band_bias",
    )(rel_bias)


def _band_attn_kernel(bias_ref, q_ref, k_ref, v_ref, o_ref, kp_ref, vp_ref, *, seq):
    kp_ref[0:BAND_PAD, :] = jnp.zeros((BAND_PAD, HEAD_DV), BF16)
    vp_ref[0:BAND_PAD, :] = jnp.zeros((BAND_PAD, HEAD_DV), BF16)
    kp_ref[BAND_PAD:BAND_PAD + seq, :] = k_ref[...]
    vp_ref[BAND_PAD:BAND_PAD + seq, :] = v_ref[...]
    col = lax.broadcasted_iota(jnp.int32, (BAND_TQ, BAND_TK), 1)
    scale = HEAD_DV ** -0.5

    def tile(t, _):
        r0 = pl.multiple_of(t * BAND_TQ, BAND_TQ)
        q = q_ref[pl.ds(r0, BAND_TQ), :]
        kb = kp_ref[pl.ds(r0, BAND_TK), :]
        vb = vp_ref[pl.ds(r0, BAND_TK), :]
        s = _nt(q, kb) * scale + bias_ref[0, 0]
        s = jnp.where(col + r0 >= BAND_PAD, s, NEG)
        p = jnp.exp(s - jnp.max(s, axis=-1, keepdims=True))
        o = _mm(p.astype(BF16), vb) / jnp.sum(p, axis=-1, keepdims=True)
        o_ref[pl.ds(r0, BAND_TQ), :] = o.astype(o_ref.dtype)
        return 0

    lax.fori_loop(0, seq // BAND_TQ, tile, 0, unroll=4)


def band_attention(u_mm, bias, layer, batch, seq):
    cq, ck, cv = MM_CA_Q // LANES, MM_CA_K // LANES, MM_CA_V // LANES
    kern = functools.partial(_band_attn_kernel, seq=seq)
    return pl.pallas_call(
        kern,
        grid=(batch, N_HEADS),
        in_specs=[
            pl.BlockSpec((1, 1, BAND_TQ, BAND_TK), lambda b, h: (layer, h, 0, 0)),
            pl.BlockSpec((seq, LANES), lambda b, h: (b, cq + h)),
            pl.BlockSpec((seq, LANES), lambda b, h: (b, ck + h)),
            pl.BlockSpec((seq, LANES), lambda b, h: (b, cv + h)),
        ],
        out_specs=pl.BlockSpec((seq, HEAD_DV), lambda b, h: (b, h)),
        out_shape=jax.ShapeDtypeStruct((batch * seq, BRANCH_WIDTH), BF16),
        scratch_shapes=[pltpu.VMEM((BAND_PAD + seq, HEAD_DV), BF16),
                        pltpu.VMEM((BAND_PAD + seq, HEAD_DV), BF16)],
        compiler_params=_params("parallel", "parallel"),
        name="band_attention",
    )(bias, u_mm, u_mm, u_mm)


def _gated_merge_kernel(h_ref, ba_ref, bb_ref, bc_ref, bd_ref, wg_ref, bg_ref, wb_ref, o_ref):
    h = h_ref[...]
    merged = None
    for bi, br_ref in enumerate((ba_ref, bb_ref, bc_ref, bd_ref)):
        gate = _sigmoid(_mm(h, wg_ref[bi]) + bg_ref[bi])
        term = gate * _mm(br_ref[...], wb_ref[bi])
        merged = term if merged is None else merged + term
    o_ref[...] = merged.astype(o_ref.dtype)


def gated_merge(h, branches, w_gate, b_gate, w_branch, tm=1024, tn=512):
    m, d = h.shape
    w = BRANCH_WIDTH
    tm = min(tm, m)
    row_d = pl.BlockSpec((tm, d), lambda i, j: (i, 0))
    row_w = pl.BlockSpec((tm, w), lambda i, j: (i, 0))
    return pl.pallas_call(
        _gated_merge_kernel,
        grid=(m // tm, d // tn),
        in_specs=[
            row_d, row_w, row_w, row_w, row_w,
            pl.BlockSpec((4, d, tn), lambda i, j: (0, 0, j)),
            pl.BlockSpec((4, 1, tn), lambda i, j: (0, 0, j)),
            pl.BlockSpec((4, w, tn), lambda i, j: (0, 0, j)),
        ],
        out_specs=pl.BlockSpec((tm, tn), lambda i, j: (i, j)),
        out_shape=jax.ShapeDtypeStruct((m, d), BF16),
        compiler_params=_params("parallel", "arbitrary"),
        name="gated_merge",
    )(h, *branches, w_gate, b_gate.reshape(4, 1, d), w_branch)


def _out_proj_kernel(a_ref, x_ref, wo_ref, gpost_ref, gnext_ref, xo_ref, ho_ref):
    x_new = x_ref[...] + _rms(_mm(a_ref[...], wo_ref[...]), gpost_ref[...])
    xo_ref[...] = x_new
    ho_ref[...] = _rms(x_new, gnext_ref[...]).astype(ho_ref.dtype)


def out_proj(a, x, w_out, g_post, g_next, tm=512):
    m, d = x.shape
    row_d = pl.BlockSpec((tm, d), lambda i: (i, 0))
    vec_d = pl.BlockSpec((1, d), lambda i: (0, 0))
    return pl.pallas_call(
        _out_proj_kernel,
        grid=(m // tm,),
        in_specs=[row_d, row_d, pl.BlockSpec((d, d), lambda i: (0, 0)), vec_d, vec_d],
        out_specs=[row_d, row_d],
        out_shape=[jax.ShapeDtypeStruct((m, d), F32), jax.ShapeDtypeStruct((m, d), BF16)],
        compiler_params=_params("parallel"),
        name="out_proj",
    )(a, x, w_out, g_post.reshape(1, d), g_next.reshape(1, d))


def _ffn_kernel(h_ref, x_ref, wu_ref, wd_ref, gpost_ref, *rest, emit_next):
    if emit_next:
        gnext_ref, xo_ref, ho_ref = rest
    else:
        (xo_ref,) = rest
    j = pl.program_id(1)

    @pl.when(j == 0)
    def _():
        xo_ref[...] = jnp.zeros_like(xo_ref)

    act = jnp.square(jnp.maximum(_mm(h_ref[...], wu_ref[...]), 0.0))
    xo_ref[...] += _mm(act.astype(BF16), wd_ref[...])

    @pl.when(j == pl.num_programs(1) - 1)
    def _():
        x_new = x_ref[...] + _rms(xo_ref[...], gpost_ref[...])
        xo_ref[...] = x_new
        if emit_next:
            ho_ref[...] = _rms(x_new, gnext_ref[...]).astype(ho_ref.dtype)


def ffn(h, x, w_up, w_down, g_post, g_next=None, tm=1024, tf=512):
    m, d = x.shape
    f = w_up.shape[1]
    tm = min(tm, m)
    emit_next = g_next is not None
    row_d = pl.BlockSpec((tm, d), lambda i, j: (i, 0))
    vec_d = pl.BlockSpec((1, d), lambda i, j: (0, 0))
    x_spec = pl.BlockSpec((tm, d), lambda i, j: (i, 0), pipeline_mode=pl.Buffered(1))
    in_specs = [row_d, x_spec, pl.BlockSpec((d, tf), lambda i, j: (0, j)), pl.BlockSpec((tf, d), lambda i, j: (j, 0)),
                vec_d]
    args = [h, x, w_up, w_down, g_post.reshape(1, d)]
    out_specs = [row_d]
    out_shape = [jax.ShapeDtypeStruct((m, d), F32)]
    if emit_next:
        in_specs.append(vec_d)
        args.append(g_next.reshape(1, d))
        out_specs.append(row_d)
        out_shape.append(jax.ShapeDtypeStruct((m, d), BF16))
    return pl.pallas_call(
        functools.partial(_ffn_kernel, emit_next=emit_next),
        grid=(m // tm, f // tf),
        in_specs=in_specs,
        out_specs=out_specs,
        out_shape=out_shape,
        compiler_params=_params("parallel", "arbitrary"),
        name="ffn",
    )(*args)


def _pack_w_in(w_in_l):
    w = BRANCH_WIDTH
    gla0 = 3 * w
    ml0 = gla0 + 2 * N_HEADS * GLA_DK + w + GLA_GATE_RANK + w
    ca0 = ml0 + 4 * w + 2 * N_HEADS
    c = lambda a, b: w_in_l[:, a:b]
    g_low0 = gla0 + 2 * N_HEADS * GLA_DK + w
    gates = jnp.concatenate([c(g_low0, g_low0 + GLA_GATE_RANK), c(ml0 + 4 * w, ml0 + 4 * w + 2 * N_HEADS)], axis=1)
    gates = jnp.pad(gates, ((0, 0), (0, LANES - gates.shape[1])))
    w_mm = jnp.concatenate([c(0, 3 * w), c(gla0, gla0 + 2 * N_HEADS * GLA_DK + w),
                            c(ml0 + 2 * w, ml0 + 3 * w), c(ca0, ca0 + 3 * w)], axis=1)
    w_fx = jnp.concatenate([c(ml0, ml0 + 2 * w), c(g_low0 + GLA_GATE_RANK, g_low0 + GLA_GATE_RANK + w),
                            c(ml0 + 3 * w, ml0 + 4 * w), gates], axis=1)
    return w_mm, w_fx


def kernel(x, norm_mix_pre, norm_mix_post, w_in, diff_lambda, diff_norm, gla_w_gate_up, gla_b_gate, gla_norm,
           mlstm_conv_w, mlstm_conv_b, mlstm_b_i, mlstm_b_f, mlstm_norm, rel_bias, w_branch, w_gate, b_gate,
           w_out, norm_ffn_pre, norm_ffn_post, w_up, w_down):
    batch, seq, d = x.shape
    depth = w_in.shape[0]
    m = batch * seq
    xf = x.reshape(m, d)
    bias_tiles = band_bias(rel_bias)
    h = rmsnorm_cast(xf, norm_mix_pre[0])
    for l in range(depth):
        w_mm, w_fx = (a.astype(BF16) for a in _pack_w_in(w_in[l]))
        u_mm = matmul(h, w_mm, BF16, tm=min(1024, m), tn=MM_COLS // 3, name="in_proj_mm")
        u_fx = matmul(h, w_fx, F32, tm=512, tn=FX_COLS, name="in_proj_fx")
        o_gla, o_mlstm = recurrent_mixers(u_mm, u_fx, gla_w_gate_up[l], gla_b_gate[l], gla_norm[l], mlstm_conv_w[l],
                                          mlstm_conv_b[l], mlstm_b_i[l], mlstm_b_f[l], mlstm_norm[l], batch, seq)
        branches = (
            diff_attention(u_mm, diff_lambda[l], diff_norm[l], l, batch, seq),
            o_gla,
            o_mlstm,
            band_attention(u_mm, bias_tiles, l, batch, seq),
        )
        merged = gated_merge(h, branches, w_gate[l].astype(BF16), b_gate[l], w_branch[l].astype(BF16))
        xf, h2 = out_proj(merged, xf, w_out[l].astype(BF16), norm_mix_post[l], norm_ffn_pre[l])
        g_next = norm_mix_pre[l + 1] if l + 1 < depth else None
        outs = ffn(h2, xf, w_up[l].astype(BF16), w_down[l].astype(BF16), norm_ffn_post[l], g_next)
        xf, h = outs if g_next is not None else (outs[0], None)
    return xf.reshape(batch, seq, d)
```

```python
import functools
import math

import jax
import jax.numpy as jnp
from jax import lax
from jax.experimental import pallas as pl
from jax.experimental.pallas import tpu as pltpu

F32 = jnp.float32
BF16 = jnp.bfloat16

D_MODEL = 2048
CHUNK = 64
BRANCH_WIDTH = 512
N_HEADS = 4
HEAD_DV = 128
DIFF_QK_DIM = 64
GLA_DK = 64
GLA_GATE_RANK = 16
GLA_GATE_NORM = 16.0
MLSTM_CONV = 4
CA_LEFT_CHUNKS = 8
REL_CLIP = 2 * CHUNK
D_FF = 4 * D_MODEL
EPS = 1e-6
NEG = -1e30

LANES = 128
VMEM_LIMIT = 56 * 1024 * 1024
FFN_VMEM_LIMIT = 60 * 1024 * 1024

MM_DIFF_Q, MM_DIFF_K, MM_DIFF_V = 0, 512, 1024
MM_GLA_Q, MM_GLA_K, MM_GLA_V = 1536, 1792, 2048
MM_ML_V = 2560
MM_CA_Q, MM_CA_K, MM_CA_V = 3072, 3584, 4096
MM_COLS = 4608
FX_ML_QK, FX_GLA_R, FX_ML_O, FX_GATES = 0, 1024, 1536, 2048
FX_COLS = 2176
GATE_GLA_LOW, GATE_ML_I, GATE_ML_F = 0, 16, 20

SUM_ROWS = 16

NT_DIMS = (((1,), (1,)), ((), ()))
TN_DIMS = (((0,), (0,)), ((), ()))


def _nt(a, b):
    return lax.dot_general(a, b, NT_DIMS, preferred_element_type=F32)


def _tn(a, b):
    return lax.dot_general(a, b, TN_DIMS, preferred_element_type=F32)


def _mm(a, b):
    return jnp.dot(a, b, preferred_element_type=F32)


def _tri_cumsum(tril, x):
    hi = x.astype(BF16)
    rest = x - hi.astype(F32)
    mid = rest.astype(BF16)
    lo = (rest - mid.astype(F32)).astype(BF16)
    n = x.shape[1]
    parts = _mm(tril, jnp.concatenate([hi, mid, lo], axis=1))
    return parts[:, :n] + parts[:, n:2 * n] + parts[:, 2 * n:]


def _rms(x, gain):
    return x * lax.rsqrt(jnp.mean(x * x, axis=-1, keepdims=True) + EPS) * gain


def _log_sigmoid(x):
    return jnp.minimum(x, 0.0) - jnp.log1p(jnp.exp(-jnp.abs(x)))


def _sigmoid(x):
    return 1.0 / (1.0 + jnp.exp(-x))


def _params(*sem):
    return pltpu.CompilerParams(dimension_semantics=sem, vmem_limit_bytes=VMEM_LIMIT)


def _rmsnorm_cast_kernel(x_ref, g_ref, o_ref):
    o_ref[...] = _rms(x_ref[...], g_ref[...]).astype(o_ref.dtype)


def rmsnorm_cast(x, g, tm=512):
    m, d = x.shape
    return pl.pallas_call(
        _rmsnorm_cast_kernel,
        grid=(m // tm,),
        in_specs=[pl.BlockSpec((tm, d), lambda i: (i, 0)), pl.BlockSpec((1, d), lambda i: (0, 0))],
        out_specs=pl.BlockSpec((tm, d), lambda i: (i, 0)),
        out_shape=jax.ShapeDtypeStruct((m, d), BF16),
        compiler_params=_params("parallel"),
        name="rmsnorm_cast",
    )(x, g.reshape(1, d))


def _matmul_kernel(a_ref, b_ref, o_ref):
    o_ref[...] = _mm(a_ref[...], b_ref[...]).astype(o_ref.dtype)


def matmul(a, b, out_dtype, tm, tn, name):
    m, k = a.shape
    n = b.shape[1]
    return pl.pallas_call(
        _matmul_kernel,
        grid=(m // tm, n // tn),
        in_specs=[pl.BlockSpec((tm, k), lambda i, j: (i, 0)), pl.BlockSpec((k, tn), lambda i, j: (0, j))],
        out_specs=pl.BlockSpec((tm, tn), lambda i, j: (i, j)),
        out_shape=jax.ShapeDtypeStruct((m, n), out_dtype),
        compiler_params=_params("parallel", "arbitrary"),
        name=name,
    )(a, b)


def _diff_attn_kernel(slopes_ref, lam_ref, gain_ref, q_ref, k_ref, v_ref, o_ref,
                      vt_ref, kb_ref, diag_ref, m_ref, acc_ref, *, t, lam_init):
    h = pl.program_id(1)
    qi = pl.program_id(2)
    slope = slopes_ref[h]
    q0 = pl.multiple_of(qi * t, t)
    lanes = lax.broadcasted_iota(jnp.int32, (1, LANES), 1)

    @pl.when(qi == 0)
    def _():
        vt_ref[0:HEAD_DV, :] = v_ref[...].astype(F32).T.astype(BF16)
        vt_ref[HEAD_DV:, :] = jnp.ones((SUM_ROWS, v_ref.shape[0]), BF16)
        key = lax.broadcasted_iota(jnp.int32, (t, t), 0)
        qry = lax.broadcasted_iota(jnp.int32, (t, t), 1)
        past = slope * key.astype(F32)
        visible = key < (qry // CHUNK + 1) * CHUNK
        diag = jnp.where(visible, slope * (qry - jnp.abs(qry - key)).astype(F32), NEG)
        for mi in range(2):
            kb_ref[:, mi * t:(mi + 1) * t] = past
            diag_ref[:, mi * t:(mi + 1) * t] = diag

    q = q_ref[...] * jnp.asarray(DIFF_QK_DIM ** -0.5, BF16)
    q_maps = jnp.concatenate([jnp.where(lanes < DIFF_QK_DIM, q, jnp.zeros_like(q)),
                              jnp.where(lanes >= DIFF_QK_DIM, q, jnp.zeros_like(q))], axis=0)
    m_ref[...] = jnp.full(m_ref.shape, NEG, F32)
    acc_ref[...] = jnp.zeros_like(acc_ref)

    def accumulate(z, shift, vt):
        m_prev = m_ref[...]
        m_new = jnp.maximum(m_prev, jnp.max(z, axis=0, keepdims=True) + shift)
        p = jnp.exp(z - (m_new - shift))
        acc_ref[...] = jnp.exp(m_prev - m_new) * acc_ref[...] + _mm(vt, p.astype(BF16))
        m_ref[...] = m_new

    def past_tile(j, _):
        k0 = pl.multiple_of(j * t, t)
        z = _nt(k_ref[pl.ds(k0, t), :], q_maps) + kb_ref[...]
        accumulate(z, slope * (k0 - q0).astype(F32), vt_ref[:, pl.ds(k0, t)])
        return 0

    def two_past_tiles(jj, _):
        past_tile(2 * jj, 0)
        past_tile(2 * jj + 1, 0)
        return 0

    lax.fori_loop(0, qi // 2, two_past_tiles, 0)

    @pl.when(qi % 2 == 1)
    def _():
        past_tile(qi - 1, 0)

    z = _nt(k_ref[pl.ds(q0, t), :], q_maps) + diag_ref[...]
    accumulate(z, 0.0, vt_ref[:, pl.ds(q0, t)])

    lam = lam_ref[...]
    lam_full = (jnp.exp(jnp.sum(lam[0:1] * lam[1:2], axis=-1, keepdims=True))
                - jnp.exp(jnp.sum(lam[2:3] * lam[3:4], axis=-1, keepdims=True)) + lam_init)
    acc = acc_ref[...]
    ratio = acc[:HEAD_DV] / acc[HEAD_DV:HEAD_DV + 1]
    o = (ratio[:, :t] - lam_full * ratio[:, t:]).T
    o_ref[...] = (_rms(o, gain_ref[...]) * (1.0 - lam_init)).astype(o_ref.dtype)


def diff_attention(u_mm, lam, gain, layer_idx, batch, seq, tq=512):
    lam_init = 0.8 - 0.6 * math.exp(-0.3 * layer_idx)
    nq = seq // tq
    slopes = jnp.asarray([2.0 ** (-8.0 * (i + 1) / N_HEADS) for i in range(N_HEADS)], F32)
    cq, ck, cv = MM_DIFF_Q // LANES, MM_DIFF_K // LANES, MM_DIFF_V // LANES
    kern = functools.partial(_diff_attn_kernel, t=tq, lam_init=lam_init)
    return pl.pallas_call(
        kern,
        grid=(batch, N_HEADS, nq),
        in_specs=[
            pl.BlockSpec(memory_space=pltpu.SMEM),
            pl.BlockSpec((4, DIFF_QK_DIM), lambda b, h, i: (0, 0)),
            pl.BlockSpec((1, HEAD_DV), lambda b, h, i: (0, h)),
            pl.BlockSpec((tq, LANES), lambda b, h, i: (b * nq + i, cq + h)),
            pl.BlockSpec((seq, LANES), lambda b, h, i: (b, ck + h)),
            pl.BlockSpec((seq, LANES), lambda b, h, i: (b, cv + h)),
        ],
        out_specs=pl.BlockSpec((tq, HEAD_DV), lambda b, h, i: (b * nq + i, h)),
        out_shape=jax.ShapeDtypeStruct((batch * seq, BRANCH_WIDTH), BF16),
        scratch_shapes=[pltpu.VMEM((HEAD_DV + SUM_ROWS, seq), BF16),
                        pltpu.VMEM((tq, 2 * tq), F32),
                        pltpu.VMEM((tq, 2 * tq), F32),
                        pltpu.VMEM((1, 2 * tq), F32),
                        pltpu.VMEM((HEAD_DV + SUM_ROWS, 2 * tq), F32)],
        compiler_params=_params("parallel", "parallel", "arbitrary"),
        name="diff_attention",
    )(slopes, lam, gain.reshape(1, BRANCH_WIDTH), u_mm, u_mm, u_mm)


def _recurrent_mixers_kernel(
        g_ref,
        gq_ref, gk_ref, gv_ref, gr_ref, wgu_ref, bg_ref, ggain_ref,
        qk_ref, v_ref, op_ref, cw_ref, cb_ref, gb_ref, gain_ref,
        go_ref, o_ref,
        st_ref, c_ref, n_ref, m_ref, halo_ref, *, tt):
    w = BRANCH_WIDTH
    halo = halo_ref.shape[0]

    @pl.when(pl.program_id(1) == 0)
    def _():
        st_ref[...] = jnp.zeros_like(st_ref)
        c_ref[...] = jnp.zeros_like(c_ref)
        n_ref[...] = jnp.zeros_like(n_ref)
        m_ref[...] = jnp.zeros_like(m_ref)
        halo_ref[...] = jnp.zeros_like(halo_ref)

    dkh = N_HEADS * GLA_DK
    hc = N_HEADS * CHUNK
    iota = lambda shape, dim: lax.broadcasted_iota(jnp.int32, shape, dim)
    tril = (iota((CHUNK, CHUNK), 0) >= iota((CHUNK, CHUNK), 1)).astype(BF16)
    lane = iota((1, LANES), 1)
    head_of_dk_lane = iota((1, dkh), 1) // GLA_DK
    head_of_dv_lane = iota((1, w), 1) // HEAD_DV
    gla_causal = iota((CHUNK, hc), 0) >= iota((CHUNK, hc), 1) % CHUNK
    gla_state_block = iota((w, dkh), 0) // HEAD_DV == iota((w, dkh), 1) // GLA_DK
    r_i, c_i = iota((hc, hc), 0), iota((hc, hc), 1)
    ml_visible = (r_i // CHUNK == c_i // CHUNK) & (r_i % CHUNK >= c_i % CHUNK)

    def per_head_rows(a, width):
        return jnp.concatenate([a[:, h * width:(h + 1) * width] for h in range(N_HEADS)], axis=0)

    def head_blocks(a, head_of_lane):
        return jnp.concatenate([jnp.where(head_of_lane == h, a, jnp.zeros_like(a)) for h in range(N_HEADS)], axis=0)

    def gla_chunk(c):
        r0 = pl.multiple_of(c * CHUNK, CHUNK)
        rows = pl.ds(r0, CHUNK)
        pre = _mm(g_ref[rows, :].astype(BF16), wgu_ref[...]) + bg_ref[...]
        log_a = _log_sigmoid(pre) * (1.0 / GLA_GATE_NORM)
        cum = _tri_cumsum(tril, log_a)
        total = cum[CHUNK - 1:CHUNK, :]
        qf = gq_ref[rows, :].astype(F32) * (GLA_DK ** -0.5)
        kf = gk_ref[rows, :].astype(F32)
        q_dec = (qf * jnp.exp(cum)).astype(BF16)
        k_inv = (kf * jnp.exp(-cum)).astype(BF16)
        k_tail = (kf * jnp.exp(total - cum)).astype(BF16)
        v = gv_ref[rows, :]
        att = jnp.where(gla_causal, _nt(q_dec, head_blocks(k_inv, head_of_dk_lane)), 0.0)
        st = st_ref[...]
        o = _mm(att.astype(BF16), head_blocks(v, head_of_dv_lane)) + _nt(q_dec, st.astype(BF16))
        st_ref[...] = jnp.exp(total) * st + jnp.where(gla_state_block, _tn(v, k_tail), 0.0)
        for h in range(N_HEADS):
            vs = slice(h * HEAD_DV, (h + 1) * HEAD_DV)
            r = gr_ref[rows, vs]
            y = _rms(o[:, vs], ggain_ref[:, vs]) * (r * _sigmoid(r))
            go_ref[rows, vs] = y.astype(go_ref.dtype)

    def mlstm_chunk(c):
        r0 = pl.multiple_of(c * CHUNK, CHUNK)
        rows = pl.ds(r0, CHUNK)
        p0 = pl.multiple_of(jnp.maximum(r0 - halo, 0), halo)
        prev = jnp.where(c > 0, qk_ref[pl.ds(p0, halo), :], halo_ref[...])
        ext = jnp.concatenate([prev, qk_ref[rows, :]], axis=0)
        acc = cb_ref[...]
        for j in range(MLSTM_CONV):
            s0 = halo - (MLSTM_CONV - 1) + j
            acc = acc + cw_ref[j:j + 1, :] * ext[s0:s0 + CHUNK, :]
        qk = acc * _sigmoid(acc)
        gb = g_ref[rows, :] + gb_ref[...]
        log_f = _log_sigmoid(gb)
        bcum = _tri_cumsum(tril, log_f)
        x = jnp.where(lane < GATE_ML_F, gb, bcum)
        xt = x.T
        heads = range(N_HEADS)
        rep = lambda per_head: jnp.concatenate(
            [jnp.broadcast_to(a, (CHUNK, a.shape[-1])) for a in per_head], axis=0)
        bcum_c = jnp.concatenate([bcum[:, GATE_ML_F + h:GATE_ML_F + h + 1] for h in heads], axis=0)
        logi_c = jnp.concatenate([gb[:, GATE_ML_I + h:GATE_ML_I + h + 1] for h in heads], axis=0)
        r_row = jnp.concatenate([xt[GATE_ML_I + h:GATE_ML_I + h + 1, :] - xt[GATE_ML_F + h:GATE_ML_F + h + 1, :]
                                 for h in heads], axis=1)
        btot = [bcum[CHUNK - 1:CHUNK, GATE_ML_F + h:GATE_ML_F + h + 1] for h in heads]
        m_prev = [m_ref[h] for h in heads]
        tail = rep(btot) - bcum_c + logi_c
        m_loc = [jnp.max(tail[h * CHUNK:(h + 1) * CHUNK], axis=0, keepdims=True) for h in heads]
        w_tail = jnp.exp(tail - rep(m_loc))
        q_lanes = qk[:, :w].astype(BF16)
        q = per_head_rows(q_lanes, HEAD_DV)
        kf = per_head_rows(qk[:, w:], HEAD_DV) * (HEAD_DV ** -0.5)
        v = per_head_rows(v_ref[rows, :], HEAD_DV)
        kw = kf * w_tail
        dlog = jnp.where(ml_visible, bcum_c + r_row, NEG)
        inter_log = bcum_c + rep(m_prev)
        m_t = jnp.maximum(jnp.max(dlog, axis=-1, keepdims=True), inter_log)
        a_mat = jnp.exp(dlog - m_t) * _nt(q, kf.astype(BF16))
        inter_w = jnp.exp(inter_log - m_t)
        c_prev = c_ref[...]
        n_prev = [n_ref[h] for h in heads]
        num = _mm(a_mat.astype(BF16), v) + inter_w * _mm(head_blocks(q_lanes, head_of_dv_lane), c_prev.astype(BF16))
        qn = jnp.sum(q.astype(F32) * rep(n_prev), axis=-1, keepdims=True)
        den = jnp.sum(a_mat, axis=-1, keepdims=True) + inter_w * qn
        hh = num / jnp.maximum(jnp.abs(den), jnp.exp(-m_t))
        for h in heads:
            hs = slice(h * HEAD_DV, (h + 1) * HEAD_DV)
            y = _rms(_sigmoid(op_ref[rows, hs]) * hh[h * CHUNK:(h + 1) * CHUNK], gain_ref[:, hs])
            o_ref[rows, hs] = y.astype(o_ref.dtype)
        kw_lanes = jnp.concatenate([kw[h * CHUNK:(h + 1) * CHUNK] for h in heads], axis=1)
        d_c = _tn(head_blocks(kw_lanes.astype(BF16), head_of_dv_lane), v)
        rep_d = lambda per_head: jnp.concatenate(
            [jnp.broadcast_to(a, (HEAD_DV, 1)) for a in per_head], axis=0)
        m_new = [jnp.maximum(btot[h] + m_prev[h], m_loc[h]) for h in heads]
        a = [jnp.exp(btot[h] + m_prev[h] - m_new[h]) for h in heads]
        bb = [jnp.exp(m_loc[h] - m_new[h]) for h in heads]
        c_ref[...] = rep_d(a) * c_prev + rep_d(bb) * d_c
        for h in heads:
            d_n = jnp.sum(kw[h * CHUNK:(h + 1) * CHUNK], axis=0, keepdims=True)
            n_ref[h] = a[h] * n_prev[h] + bb[h] * d_n
            m_ref[h] = m_new[h]

    def chunk(c, _):
        gla_chunk(c)
        mlstm_chunk(c)
        return 0

    lax.fori_loop(0, tt // CHUNK, chunk, 0, unroll=4)
    halo_ref[...] = qk_ref[tt - halo:tt, :]


def recurrent_mixers(u_mm, u_fx, gla_w_gate_up, gla_b_gate, gla_gain, conv_w, conv_b, b_i, b_f, ml_gain,
                     batch, seq, tt=1024):
    w = BRANCH_WIDTH
    dkh = N_HEADS * GLA_DK
    tt = min(tt, seq)
    nt = seq // tt
    wgu = (jnp.zeros((LANES, dkh), F32).at[GATE_GLA_LOW:GATE_GLA_LOW + GLA_GATE_RANK].set(gla_w_gate_up)
           .astype(BF16))
    gate_bias = (jnp.zeros((1, LANES), F32).at[0, GATE_ML_I:GATE_ML_I + N_HEADS].set(b_i)
                 .at[0, GATE_ML_F:GATE_ML_F + N_HEADS].set(b_f))
    const = lambda b, t: (0, 0)
    rows = lambda width, col0: pl.BlockSpec((tt, width), lambda b, t: (b * nt + t, col0 // width))
    out = jax.ShapeDtypeStruct((batch * seq, w), BF16)
    return pl.pallas_call(
        functools.partial(_recurrent_mixers_kernel, tt=tt),
        grid=(batch, nt),
        in_specs=[
            rows(LANES, FX_GATES),
            rows(dkh, MM_GLA_Q), rows(dkh, MM_GLA_K), rows(w, MM_GLA_V), rows(w, FX_GLA_R),
            pl.BlockSpec((LANES, dkh), const), pl.BlockSpec((1, dkh), const), pl.BlockSpec((1, w), const),
            rows(2 * w, FX_ML_QK), rows(w, MM_ML_V), rows(w, FX_ML_O),
            pl.BlockSpec((MLSTM_CONV, 2 * w), const), pl.BlockSpec((1, 2 * w), const),
            pl.BlockSpec((1, LANES), const), pl.BlockSpec((1, w), const),
        ],
        out_specs=[rows(w, 0), rows(w, 0)],
        out_shape=[out, out],
        scratch_shapes=[pltpu.VMEM((w, dkh), F32),
                        pltpu.VMEM((N_HEADS * HEAD_DV, HEAD_DV), F32),
                        pltpu.VMEM((N_HEADS, 1, HEAD_DV), F32),
                        pltpu.VMEM((N_HEADS, 1, 1), F32),
                        pltpu.VMEM((8, 2 * w), F32)],
        compiler_params=_params("parallel", "arbitrary"),
        name="recurrent_mixers",
    )(u_fx,
      u_mm, u_mm, u_mm, u_fx, wgu, gla_b_gate.reshape(1, dkh), gla_gain.reshape(1, w),
      u_fx, u_mm, u_fx, conv_w, conv_b.reshape(1, 2 * w), gate_bias, ml_gain.reshape(1, w))


BAND_Q_CHUNKS = 4
BAND_TQ = BAND_Q_CHUNKS * CHUNK
BAND_TK = (CA_LEFT_CHUNKS + BAND_Q_CHUNKS) * CHUNK
BAND_PAD = CA_LEFT_CHUNKS * CHUNK


def _band_bias_kernel(table_ref, o_ref):
    l = pl.program_id(0)
    h = pl.program_id(1)
    row = lax.broadcasted_iota(jnp.int32, (BAND_TQ, BAND_TK), 0)
    col = lax.broadcasted_iota(jnp.int32, (BAND_TQ, BAND_TK), 1)
    idx = jnp.clip(row - col + BAND_PAD, -REL_CLIP, REL_CLIP) + REL_CLIP
    delta = row // CHUNK + CA_LEFT_CHUNKS - col // CHUNK
    in_band = (delta >= 0) & (delta <= CA_LEFT_CHUNKS)

    far = BAND_PAD - REL_CLIP
    near_idx = idx[:, far:]

    def body(r, acc):
        return jnp.where(near_idx == r, table_ref[l, h, r], acc)

    near = lax.fori_loop(0, 2 * REL_CLIP + 1, body, jnp.zeros(near_idx.shape, F32))
    bias = jnp.concatenate([jnp.full((BAND_TQ, far), table_ref[l, h, 2 * REL_CLIP], F32), near], axis=1)
    o_ref[0, 0] = jnp.where(in_band, bias, NEG)


def band_bias(rel_bias):
    depth = rel_bias.shape[0]
    return pl.pallas_call(
        _band_bias_kernel,
        grid=(depth, N_HEADS),
        in_specs=[pl.BlockSpec(memory_space=pltpu.SMEM)],
        out_specs=pl.BlockSpec((1, 1, BAND_TQ, BAND_TK), lambda l, h: (l, h, 0, 0)),
        out_shape=jax.ShapeDtypeStruct((depth, N_HEADS, BAND_TQ, BAND_TK), F32),
        compiler_params=_params("parallel", "parallel"),
        name="band_bias",
    )(rel_bias)


def _band_attn_kernel(bias_ref, q_ref, k_ref, v_ref, o_ref, kp_ref, vp_ref, *, seq):
    kp_ref[0:BAND_PAD, :] = jnp.zeros((BAND_PAD, HEAD_DV), BF16)
    vp_ref[0:BAND_PAD, :] = jnp.zeros((BAND_PAD, HEAD_DV), BF16)
    kp_ref[BAND_PAD:BAND_PAD + seq, :] = k_ref[...]
    vp_ref[BAND_PAD:BAND_PAD + seq, :] = v_ref[...]
    col = lax.broadcasted_iota(jnp.int32, (BAND_TQ, BAND_TK), 1)
    scale = HEAD_DV ** -0.5

    def tile(t, _):
        r0 = pl.multiple_of(t * BAND_TQ, BAND_TQ)
        q = q_ref[pl.ds(r0, BAND_TQ), :]
        kb = kp_ref[pl.ds(r0, BAND_TK), :]
        vb = vp_ref[pl.ds(r0, BAND_TK), :]
        s = _nt(q, kb) * scale + bias_ref[0, 0]
        s = jnp.where(col + r0 >= BAND_PAD, s, NEG)
        p = jnp.exp(s - jnp.max(s, axis=-1, keepdims=True))
        o = _mm(p.astype(BF16), vb) / jnp.sum(p, axis=-1, keepdims=True)
        o_ref[pl.ds(r0, BAND_TQ), :] = o.astype(o_ref.dtype)
        return 0

    lax.fori_loop(0, seq // BAND_TQ, tile, 0, unroll=4)


def band_attention(u_mm, bias, layer, batch, seq):
    cq, ck, cv = MM_CA_Q // LANES, MM_CA_K // LANES, MM_CA_V // LANES
    kern = functools.partial(_band_attn_kernel, seq=seq)
    return pl.pallas_call(
        kern,
        grid=(batch, N_HEADS),
        in_specs=[
            pl.BlockSpec((1, 1, BAND_TQ, BAND_TK), lambda b, h: (layer, h, 0, 0)),
            pl.BlockSpec((seq, LANES), lambda b, h: (b, cq + h)),
            pl.BlockSpec((seq, LANES), lambda b, h: (b, ck + h)),
            pl.BlockSpec((seq, LANES), lambda b, h: (b, cv + h)),
        ],
        out_specs=pl.BlockSpec((seq, HEAD_DV), lambda b, h: (b, h)),
        out_shape=jax.ShapeDtypeStruct((batch * seq, BRANCH_WIDTH), BF16),
        scratch_shapes=[pltpu.VMEM((BAND_PAD + seq, HEAD_DV), BF16),
                        pltpu.VMEM((BAND_PAD + seq, HEAD_DV), BF16)],
        compiler_params=_params("parallel", "parallel"),
        name="band_attention",
    )(bias, u_mm, u_mm, u_mm)


def _gated_merge_kernel(h_ref, ba_ref, bb_ref, bc_ref, bd_ref, wg_ref, bg_ref, wb_ref, o_ref):
    h = h_ref[...]
    merged = None
    for bi, br_ref in enumerate((ba_ref, bb_ref, bc_ref, bd_ref)):
        gate = _sigmoid(_mm(h, wg_ref[bi]) + bg_ref[bi])
        term = gate * _mm(br_ref[...], wb_ref[bi])
        merged = term if merged is None else merged + term
    o_ref[...] = merged.astype(o_ref.dtype)


def gated_merge(h, branches, w_gate, b_gate, w_branch, tm=1024, tn=512):
    m, d = h.shape
    w = BRANCH_WIDTH
    tm = min(tm, m)
    row_d = pl.BlockSpec((tm, d), lambda i, j: (i, 0))
    row_w = pl.BlockSpec((tm, w), lambda i, j: (i, 0))
    return pl.pallas_call(
        _gated_merge_kernel,
        grid=(m // tm, d // tn),
        in_specs=[
            row_d, row_w, row_w, row_w, row_w,
            pl.BlockSpec((4, d, tn), lambda i, j: (0, 0, j)),
            pl.BlockSpec((4, 1, tn), lambda i, j: (0, 0, j)),
            pl.BlockSpec((4, w, tn), lambda i, j: (0, 0, j)),
        ],
        out_specs=pl.BlockSpec((tm, tn), lambda i, j: (i, j)),
        out_shape=jax.ShapeDtypeStruct((m, d), BF16),
        compiler_params=_params("parallel", "arbitrary"),
        name="gated_merge",
    )(h, *branches, w_gate, b_gate.reshape(4, 1, d), w_branch)


def _out_proj_kernel(a_ref, x_ref, wo_ref, gpost_ref, gnext_ref, xo_ref, ho_ref):
    x_new = x_ref[...] + _rms(_mm(a_ref[...], wo_ref[...]), gpost_ref[...])
    xo_ref[...] = x_new
    ho_ref[...] = _rms(x_new, gnext_ref[...]).astype(ho_ref.dtype)


def out_proj(a, x, w_out, g_post, g_next, tm=512):
    m, d = x.shape
    row_d = pl.BlockSpec((tm, d), lambda i: (i, 0))
    vec_d = pl.BlockSpec((1, d), lambda i: (0, 0))
    return pl.pallas_call(
        _out_proj_kernel,
        grid=(m // tm,),
        in_specs=[row_d, row_d, pl.BlockSpec((d, d), lambda i: (0, 0)), vec_d, vec_d],
        out_specs=[row_d, row_d],
        out_shape=[jax.ShapeDtypeStruct((m, d), F32), jax.ShapeDtypeStruct((m, d), BF16)],
        compiler_params=_params("parallel"),
        name="out_proj",
    )(a, x, w_out, g_post.reshape(1, d), g_next.reshape(1, d))


def _ffn_kernel(h_ref, x_ref, wu_ref, wd_ref, gpost_ref, *rest, emit_next, sub):
    if emit_next:
        gnext_ref, xo_ref, ho_ref = rest
    else:
        (xo_ref,) = rest
    j = pl.program_id(1)

    @pl.when(j == 0)
    def _():
        xo_ref[...] = jnp.zeros_like(xo_ref)

    for s in range(wu_ref.shape[1] // sub):
        cols = slice(s * sub, (s + 1) * sub)
        act = jnp.square(jnp.maximum(_mm(h_ref[...], wu_ref[:, cols]), 0.0))
        xo_ref[...] += _mm(act.astype(BF16), wd_ref[cols, :])

    @pl.when(j == pl.num_programs(1) - 1)
    def _():
        x_new = x_ref[...] + _rms(xo_ref[...], gpost_ref[...])
        xo_ref[...] = x_new
        if emit_next:
            ho_ref[...] = _rms(x_new, gnext_ref[...]).astype(ho_ref.dtype)


def ffn(h, x, w_up, w_down, g_post, g_next=None, tm=1024, tf=1024, sub=512):
    m, d = x.shape
    f = w_up.shape[1]
    tm = min(tm, m)
    emit_next = g_next is not None
    row_d = pl.BlockSpec((tm, d), lambda i, j: (i, 0))
    vec_d = pl.BlockSpec((1, d), lambda i, j: (0, 0))
    x_spec = pl.BlockSpec((tm, d), lambda i, j: (i, 0), pipeline_mode=pl.Buffered(1))
    in_specs = [row_d, x_spec, pl.BlockSpec((d, tf), lambda i, j: (0, j)), pl.BlockSpec((tf, d), lambda i, j: (j, 0)),
                vec_d]
    args = [h, x, w_up, w_down, g_post.reshape(1, d)]
    out_specs = [row_d]
    out_shape = [jax.ShapeDtypeStruct((m, d), F32)]
    if emit_next:
        in_specs.append(vec_d)
        args.append(g_next.reshape(1, d))
        out_specs.append(row_d)
        out_shape.append(jax.ShapeDtypeStruct((m, d), BF16))
    return pl.pallas_call(
        functools.partial(_ffn_kernel, emit_next=emit_next, sub=sub),
        grid=(m // tm, f // tf),
        in_specs=in_specs,
        out_specs=out_specs,
        out_shape=out_shape,
        compiler_params=pltpu.CompilerParams(dimension_semantics=("parallel", "arbitrary"),
                                             vmem_limit_bytes=FFN_VMEM_LIMIT),
        name="ffn",
    )(*args)


def _pack_w_in(w_in_l):
    w = BRANCH_WIDTH
    gla0 = 3 * w
    ml0 = gla0 + 2 * N_HEADS * GLA_DK + w + GLA_GATE_RANK + w
    ca0 = ml0 + 4 * w + 2 * N_HEADS
    c = lambda a, b: w_in_l[:, a:b]
    g_low0 = gla0 + 2 * N_HEADS * GLA_DK + w
    gates = jnp.concatenate([c(g_low0, g_low0 + GLA_GATE_RANK), c(ml0 + 4 * w, ml0 + 4 * w + 2 * N_HEADS)], axis=1)
    gates = jnp.pad(gates, ((0, 0), (0, LANES - gates.shape[1])))
    w_mm = jnp.concatenate([c(0, 3 * w), c(gla0, gla0 + 2 * N_HEADS * GLA_DK + w),
                            c(ml0 + 2 * w, ml0 + 3 * w), c(ca0, ca0 + 3 * w)], axis=1)
    w_fx = jnp.concatenate([c(ml0, ml0 + 2 * w), c(g_low0 + GLA_GATE_RANK, g_low0 + GLA_GATE_RANK + w),
                            c(ml0 + 3 * w, ml0 + 4 * w), gates], axis=1)
    return w_mm, w_fx


def kernel(x, norm_mix_pre, norm_mix_post, w_in, diff_lambda, diff_norm, gla_w_gate_up, gla_b_gate, gla_norm,
           mlstm_conv_w, mlstm_conv_b, mlstm_b_i, mlstm_b_f, mlstm_norm, rel_bias, w_branch, w_gate, b_gate,
           w_out, norm_ffn_pre, norm_ffn_post, w_up, w_down):
    batch, seq, d = x.shape
    depth = w_in.shape[0]
    m = batch * seq
    xf = x.reshape(m, d)
    bias_tiles = band_bias(rel_bias)
    h = rmsnorm_cast(xf, norm_mix_pre[0])
    for l in range(depth):
        w_mm, w_fx = (a.astype(BF16) for a in _pack_w_in(w_in[l]))
        u_mm = matmul(h, w_mm, BF16, tm=min(1024, m), tn=MM_COLS // 3, name="in_proj_mm")
        u_fx = matmul(h, w_fx, F32, tm=min(1024, m), tn=FX_COLS, name="in_proj_fx")
        o_gla, o_mlstm = recurrent_mixers(u_mm, u_fx, gla_w_gate_up[l], gla_b_gate[l], gla_norm[l], mlstm_conv_w[l],
                                          mlstm_conv_b[l], mlstm_b_i[l], mlstm_b_f[l], mlstm_norm[l], batch, seq)
        branches = (
            diff_attention(u_mm, diff_lambda[l], diff_norm[l], l, batch, seq),
            o_gla,
            o_mlstm,
            band_attention(u_mm, bias_tiles, l, batch, seq),
        )
        merged = gated_merge(h, branches, w_gate[l].astype(BF16), b_gate[l], w_branch[l].astype(BF16))
        xf, h2 = out_proj(merged, xf, w_out[l].astype(BF16), norm_mix_post[l], norm_ffn_pre[l])
        g_next = norm_mix_pre[l + 1] if l + 1 < depth else None
        outs = ffn(h2, xf, w_up[l].astype(BF16), w_down[l].astype(BF16), norm_ffn_post[l], g_next)
        xf, h = outs if g_next is not None else (outs[0], None)
    return xf.reshape(batch, seq, d)
```

```python
import functools
import math

import jax
import jax.numpy as jnp
from jax import lax
from jax.experimental import pallas as pl
from jax.experimental.pallas import tpu as pltpu

F32 = jnp.float32
BF16 = jnp.bfloat16

D_MODEL = 2048
CHUNK = 64
BRANCH_WIDTH = 512
N_HEADS = 4
HEAD_DV = 128
DIFF_QK_DIM = 64
GLA_DK = 64
GLA_GATE_RANK = 16
GLA_GATE_NORM = 16.0
MLSTM_CONV = 4
CA_LEFT_CHUNKS = 8
REL_CLIP = 2 * CHUNK
D_FF = 4 * D_MODEL
EPS = 1e-6
NEG = -1e30

LANES = 128
VMEM_LIMIT = 56 * 1024 * 1024
FFN_VMEM_LIMIT = 60 * 1024 * 1024

MM_DIFF_Q, MM_DIFF_K, MM_DIFF_V = 0, 512, 1024
MM_GLA_Q, MM_GLA_K, MM_GLA_V = 1536, 1792, 2048
MM_ML_V = 2560
MM_CA_Q, MM_CA_K, MM_CA_V = 3072, 3584, 4096
MM_COLS = 4608
FX_ML_QK, FX_GLA_R, FX_ML_O, FX_GATES = 0, 1024, 1536, 2048
FX_COLS = 2176
GATE_GLA_LOW, GATE_ML_I, GATE_ML_F = 0, 16, 20

SUM_ROWS = 16

NT_DIMS = (((1,), (1,)), ((), ()))
TN_DIMS = (((0,), (0,)), ((), ()))


def _nt(a, b):
    return lax.dot_general(a, b, NT_DIMS, preferred_element_type=F32)


def _tn(a, b):
    return lax.dot_general(a, b, TN_DIMS, preferred_element_type=F32)


def _mm(a, b):
    return jnp.dot(a, b, preferred_element_type=F32)


def _tri_cumsum(tril, x):
    hi = x.astype(BF16)
    rest = x - hi.astype(F32)
    mid = rest.astype(BF16)
    lo = (rest - mid.astype(F32)).astype(BF16)
    n = x.shape[1]
    parts = _mm(tril, jnp.concatenate([hi, mid, lo], axis=1))
    return parts[:, :n] + parts[:, n:2 * n] + parts[:, 2 * n:]


def _rms(x, gain):
    return x * lax.rsqrt(jnp.mean(x * x, axis=-1, keepdims=True) + EPS) * gain


def _log_sigmoid(x):
    return jnp.minimum(x, 0.0) - jnp.log1p(jnp.exp(-jnp.abs(x)))


def _sigmoid(x):
    return 0.5 * jnp.tanh(0.5 * x) + 0.5


def _params(*sem):
    return pltpu.CompilerParams(dimension_semantics=sem, vmem_limit_bytes=VMEM_LIMIT)


def _rmsnorm_cast_kernel(x_ref, g_ref, o_ref):
    o_ref[...] = _rms(x_ref[...], g_ref[...]).astype(o_ref.dtype)


def rmsnorm_cast(x, g, tm=512):
    m, d = x.shape
    return pl.pallas_call(
        _rmsnorm_cast_kernel,
        grid=(m // tm,),
        in_specs=[pl.BlockSpec((tm, d), lambda i: (i, 0)), pl.BlockSpec((1, d), lambda i: (0, 0))],
        out_specs=pl.BlockSpec((tm, d), lambda i: (i, 0)),
        out_shape=jax.ShapeDtypeStruct((m, d), BF16),
        compiler_params=_params("parallel"),
        name="rmsnorm_cast",
    )(x, g.reshape(1, d))


def _matmul_kernel(a_ref, b_ref, o_ref):
    o_ref[...] = _mm(a_ref[...], b_ref[...]).astype(o_ref.dtype)


def matmul(a, b, out_dtype, tm, tn, name):
    m, k = a.shape
    n = b.shape[1]
    return pl.pallas_call(
        _matmul_kernel,
        grid=(m // tm, n // tn),
        in_specs=[pl.BlockSpec((tm, k), lambda i, j: (i, 0)), pl.BlockSpec((k, tn), lambda i, j: (0, j))],
        out_specs=pl.BlockSpec((tm, tn), lambda i, j: (i, j)),
        out_shape=jax.ShapeDtypeStruct((m, n), out_dtype),
        compiler_params=_params("parallel", "arbitrary"),
        name=name,
    )(a, b)


def _diff_attn_kernel(slopes_ref, lam_ref, gain_ref, q_ref, k_ref, v_ref, o_ref,
                      vt_ref, kb_ref, diag_ref, m_ref, acc_ref, *, t, lam_init):
    h = pl.program_id(1)
    qi = pl.program_id(2)
    slope = slopes_ref[h]
    q0 = pl.multiple_of(qi * t, t)
    first_map =lax.broadcasted_iota(jnp.int32, (1, LANES), 1) < DIFF_QK_DIM

    @pl.when(qi == 0)
    def _():
        vt_ref[0:HEAD_DV, :] = v_ref[...].astype(F32).T.astype(BF16)
        vt_ref[HEAD_DV:, :] = jnp.ones((SUM_ROWS, v_ref.shape[0]), BF16)
        key = lax.broadcasted_iota(jnp.int32, (t, t), 0)
        qry = lax.broadcasted_iota(jnp.int32, (t, t), 1)
        past = slope * key.astype(F32)
        visible = key < (qry // CHUNK + 1) * CHUNK
        diag = jnp.where(visible, slope * (qry - jnp.abs(qry - key)).astype(F32), NEG)
        for mi in range(2):
            kb_ref[:, mi * t:(mi + 1) * t] = past
            diag_ref[:, mi * t:(mi + 1) * t] = diag

    q = q_ref[...] * jnp.asarray(DIFF_QK_DIM ** -0.5, BF16)
    q_maps = jnp.concatenate([jnp.where(first_map, q, jnp.zeros_like(q)),
                              jnp.where(first_map, jnp.zeros_like(q), q)], axis=0)
    m_ref[...] = jnp.full(m_ref.shape, NEG, F32)
    acc_ref[...] = jnp.zeros_like(acc_ref)

    def accumulate(z, shift, vt):
        m_prev = m_ref[...]
        m_new = jnp.maximum(m_prev, jnp.max(z, axis=0, keepdims=True) + shift)
        p = jnp.exp(z - (m_new - shift))
        acc_ref[...] = jnp.exp(m_prev - m_new) * acc_ref[...] + _mm(vt, p.astype(BF16))
        m_ref[...] = m_new

    def past_tile(j, _):
        k0 = pl.multiple_of(j * t, t)
        z = _nt(k_ref[pl.ds(k0, t), :], q_maps) + kb_ref[...]
        accumulate(z, slope * (k0 - q0).astype(F32), vt_ref[:, pl.ds(k0, t)])
        return 0

    def two_past_tiles(jj, _):
        past_tile(2 * jj, 0)
        past_tile(2 * jj + 1, 0)
        return 0

    lax.fori_loop(0, qi // 2, two_past_tiles, 0)

    @pl.when(qi % 2 == 1)
    def _():
        past_tile(qi - 1, 0)

    z = _nt(k_ref[pl.ds(q0, t), :], q_maps) + diag_ref[...]
    accumulate(z, 0.0, vt_ref[:, pl.ds(q0, t)])

    lam = lam_ref[...]
    lam_full = (jnp.exp(jnp.sum(lam[0:1] * lam[1:2], axis=-1, keepdims=True))
                - jnp.exp(jnp.sum(lam[2:3] * lam[3:4], axis=-1, keepdims=True)) + lam_init)
    acc = acc_ref[...]
    ratio = acc[:HEAD_DV] / acc[HEAD_DV:HEAD_DV + 1]
    o = (ratio[:, :t] - lam_full * ratio[:, t:]).T
    o_ref[...] = (_rms(o, gain_ref[...]) * (1.0 - lam_init)).astype(o_ref.dtype)


def diff_attention(u_mm, lam, gain, layer_idx, batch, seq, tq=512):
    lam_init = 0.8 - 0.6 * math.exp(-0.3 * layer_idx)
    nq = seq // tq
    slopes = jnp.asarray([2.0 ** (-8.0 * (i + 1) / N_HEADS) for i in range(N_HEADS)], F32)
    cq, ck, cv = MM_DIFF_Q // LANES, MM_DIFF_K // LANES, MM_DIFF_V // LANES
    kern = functools.partial(_diff_attn_kernel, t=tq, lam_init=lam_init)
    return pl.pallas_call(
        kern,
        grid=(batch, N_HEADS, nq),
        in_specs=[
            pl.BlockSpec(memory_space=pltpu.SMEM),
            pl.BlockSpec((4, DIFF_QK_DIM), lambda b, h, i: (0, 0)),
            pl.BlockSpec((1, HEAD_DV), lambda b, h, i: (0, h)),
            pl.BlockSpec((tq, LANES), lambda b, h, i: (b * nq + i, cq + h)),
            pl.BlockSpec((seq, LANES), lambda b, h, i: (b, ck + h)),
            pl.BlockSpec((seq, LANES), lambda b, h, i: (b, cv + h)),
        ],
        out_specs=pl.BlockSpec((tq, HEAD_DV), lambda b, h, i: (b * nq + i, h)),
        out_shape=jax.ShapeDtypeStruct((batch * seq, BRANCH_WIDTH), BF16),
        scratch_shapes=[pltpu.VMEM((HEAD_DV + SUM_ROWS, seq), BF16),
                        pltpu.VMEM((tq, 2 * tq), F32),
                        pltpu.VMEM((tq, 2 * tq), F32),
                        pltpu.VMEM((1, 2 * tq), F32),
                        pltpu.VMEM((HEAD_DV + SUM_ROWS, 2 * tq), F32)],
        compiler_params=_params("parallel", "parallel", "arbitrary"),
        name="diff_attention",
    )(slopes, lam, gain.reshape(1, BRANCH_WIDTH), u_mm, u_mm, u_mm)


def _recurrent_mixers_kernel(
        g_ref,
        gq_ref, gk_ref, gv_ref, gr_ref, wgu_ref, bg_ref, ggain_ref,
        qk_ref, v_ref, op_ref, cw_ref, cb_ref, gb_ref, gain_ref,
        go_ref, o_ref,
        st_ref, c_ref, n_ref, m_ref, halo_ref, *, tt):
    w = BRANCH_WIDTH
    halo = halo_ref.shape[0]

    @pl.when(pl.program_id(1) == 0)
    def _():
        st_ref[...] = jnp.zeros_like(st_ref)
        c_ref[...] = jnp.zeros_like(c_ref)
        n_ref[...] = jnp.zeros_like(n_ref)
        m_ref[...] = jnp.zeros_like(m_ref)
        halo_ref[...] = jnp.zeros_like(halo_ref)

    dkh = N_HEADS * GLA_DK
    hc = N_HEADS * CHUNK
    iota = lambda shape, dim: lax.broadcasted_iota(jnp.int32, shape, dim)
    tril = (iota((CHUNK, CHUNK), 0) >= iota((CHUNK, CHUNK), 1)).astype(BF16)
    lane = iota((1, LANES), 1)
    head_of_dk_lane = iota((1, dkh), 1) // GLA_DK
    head_of_dv_lane = iota((1, w), 1) // HEAD_DV
    gla_causal = iota((CHUNK, hc), 0) >= iota((CHUNK, hc), 1) % CHUNK
    gla_state_block = iota((w, dkh), 0) // HEAD_DV == iota((w, dkh), 1) // GLA_DK
    r_i, c_i = iota((hc, hc), 0), iota((hc, hc), 1)
    ml_visible = (r_i // CHUNK == c_i // CHUNK) & (r_i % CHUNK >= c_i % CHUNK)

    def per_head_rows(a, width):
        return jnp.concatenate([a[:, h * width:(h + 1) * width] for h in range(N_HEADS)], axis=0)

    def head_blocks(a, head_of_lane):
        return jnp.concatenate([jnp.where(head_of_lane == h, a, jnp.zeros_like(a)) for h in range(N_HEADS)], axis=0)

    def gla_chunk(c):
        r0 = pl.multiple_of(c * CHUNK, CHUNK)
        rows = pl.ds(r0, CHUNK)
        pre = _mm(g_ref[rows, :].astype(BF16), wgu_ref[...]) + bg_ref[...]
        log_a = _log_sigmoid(pre) * (1.0 / GLA_GATE_NORM)
        cum = _tri_cumsum(tril, log_a)
        total = cum[CHUNK - 1:CHUNK, :]
        qf = gq_ref[rows, :].astype(F32) * (GLA_DK ** -0.5)
        kf = gk_ref[rows, :].astype(F32)
        q_dec = (qf * jnp.exp(cum)).astype(BF16)
        k_inv = (kf * jnp.exp(-cum)).astype(BF16)
        k_tail = (kf * jnp.exp(total - cum)).astype(BF16)
        v = gv_ref[rows, :]
        att = jnp.where(gla_causal, _nt(q_dec, head_blocks(k_inv, head_of_dk_lane)), 0.0)
        st = st_ref[...]
        o = _mm(att.astype(BF16), head_blocks(v, head_of_dv_lane)) + _nt(q_dec, st.astype(BF16))
        st_ref[...] = jnp.exp(total) * st + jnp.where(gla_state_block, _tn(v, k_tail), 0.0)
        for h in range(N_HEADS):
            vs = slice(h * HEAD_DV, (h + 1) * HEAD_DV)
            r = gr_ref[rows, vs]
            y = _rms(o[:, vs], ggain_ref[:, vs]) * (r * _sigmoid(r))
            go_ref[rows, vs] = y.astype(go_ref.dtype)

    def mlstm_chunk(c):
        r0 = pl.multiple_of(c * CHUNK, CHUNK)
        rows = pl.ds(r0, CHUNK)
        p0 = pl.multiple_of(jnp.maximum(r0 - halo, 0), halo)
        prev = jnp.where(c > 0, qk_ref[pl.ds(p0, halo), :], halo_ref[...])
        ext = jnp.concatenate([prev, qk_ref[rows, :]], axis=0)
        acc = cb_ref[...]
        for j in range(MLSTM_CONV):
            s0 = halo - (MLSTM_CONV - 1) + j
            acc = acc + cw_ref[j:j + 1, :] * ext[s0:s0 + CHUNK, :]
        qk = acc * _sigmoid(acc)
        gb = g_ref[rows, :] + gb_ref[...]
        log_f = _log_sigmoid(gb)
        bcum = _tri_cumsum(tril, log_f)
        x = jnp.where(lane < GATE_ML_F, gb, bcum)
        xt = x.T
        heads = range(N_HEADS)
        rep = lambda per_head: jnp.concatenate(
            [jnp.broadcast_to(a, (CHUNK, a.shape[-1])) for a in per_head], axis=0)
        bcum_c = jnp.concatenate([bcum[:, GATE_ML_F + h:GATE_ML_F + h + 1] for h in heads], axis=0)
        logi_c = jnp.concatenate([gb[:, GATE_ML_I + h:GATE_ML_I + h + 1] for h in heads], axis=0)
        r_row = jnp.concatenate([xt[GATE_ML_I + h:GATE_ML_I + h + 1, :] - xt[GATE_ML_F + h:GATE_ML_F + h + 1, :]
                                 for h in heads], axis=1)
        btot = [bcum[CHUNK - 1:CHUNK, GATE_ML_F + h:GATE_ML_F + h + 1] for h in heads]
        m_prev = [m_ref[h] for h in heads]
        tail = rep(btot) - bcum_c + logi_c
        m_loc = [jnp.max(tail[h * CHUNK:(h + 1) * CHUNK], axis=0, keepdims=True) for h in heads]
        w_tail = jnp.exp(tail - rep(m_loc))
        q_lanes = qk[:, :w].astype(BF16)
        q = per_head_rows(q_lanes, HEAD_DV)
        kf = per_head_rows(qk[:, w:], HEAD_DV) * (HEAD_DV ** -0.5)
        v = per_head_rows(v_ref[rows, :], HEAD_DV)
        kw = kf * w_tail
        dlog = jnp.where(ml_visible, bcum_c + r_row, NEG)
        inter_log = bcum_c + rep(m_prev)
        m_t = jnp.maximum(jnp.max(dlog, axis=-1, keepdims=True), inter_log)
        a_mat = jnp.exp(dlog - m_t) * _nt(q, kf.astype(BF16))
        inter_w = jnp.exp(inter_log - m_t)
        c_prev = c_ref[...]
        n_prev = [n_ref[h] for h in heads]
        num = _mm(a_mat.astype(BF16), v) + inter_w * _mm(head_blocks(q_lanes, head_of_dv_lane), c_prev.astype(BF16))
        qn = jnp.sum(q.astype(F32) * rep(n_prev), axis=-1, keepdims=True)
        den = jnp.sum(a_mat, axis=-1, keepdims=True) + inter_w * qn
        hh = num / jnp.maximum(jnp.abs(den), jnp.exp(-m_t))
        for h in heads:
            hs = slice(h * HEAD_DV, (h + 1) * HEAD_DV)
            y = _rms(_sigmoid(op_ref[rows, hs]) * hh[h * CHUNK:(h + 1) * CHUNK], gain_ref[:, hs])
            o_ref[rows, hs] = y.astype(o_ref.dtype)
        kw_lanes = jnp.concatenate([kw[h * CHUNK:(h + 1) * CHUNK] for h in heads], axis=1)
        d_c = _tn(head_blocks(kw_lanes.astype(BF16), head_of_dv_lane), v)
        rep_d = lambda per_head: jnp.concatenate(
            [jnp.broadcast_to(a, (HEAD_DV, 1)) for a in per_head], axis=0)
        m_new = [jnp.maximum(btot[h] + m_prev[h], m_loc[h]) for h in heads]
        a = [jnp.exp(btot[h] + m_prev[h] - m_new[h]) for h in heads]
        bb = [jnp.exp(m_loc[h] - m_new[h]) for h in heads]
        c_ref[...] = rep_d(a) * c_prev + rep_d(bb) * d_c
        for h in heads:
            d_n = jnp.sum(kw[h * CHUNK:(h + 1) * CHUNK], axis=0, keepdims=True)
            n_ref[h] = a[h] * n_prev[h] + bb[h] * d_n
            m_ref[h] = m_new[h]

    def chunk(c, _):
        gla_chunk(c)
        mlstm_chunk(c)
        return 0

    lax.fori_loop(0, tt // CHUNK, chunk, 0, unroll=4)
    halo_ref[...] = qk_ref[tt - halo:tt, :]


def recurrent_mixers(u_mm, u_fx, gla_w_gate_up, gla_b_gate, gla_gain, conv_w, conv_b, b_i, b_f, ml_gain,
                     batch, seq, tt=1024):
    w = BRANCH_WIDTH
    dkh = N_HEADS * GLA_DK
    tt = min(tt, seq)
    nt = seq // tt
    wgu = (jnp.zeros((LANES, dkh), F32).at[GATE_GLA_LOW:GATE_GLA_LOW + GLA_GATE_RANK].set(gla_w_gate_up)
           .astype(BF16))
    gate_bias = (jnp.zeros((1, LANES), F32).at[0, GATE_ML_I:GATE_ML_I + N_HEADS].set(b_i)
                 .at[0, GATE_ML_F:GATE_ML_F + N_HEADS].set(b_f))
    const = lambda b, t: (0, 0)
    rows = lambda width, col0: pl.BlockSpec((tt, width), lambda b, t: (b * nt + t, col0 // width))
    out = jax.ShapeDtypeStruct((batch * seq, w), BF16)
    return pl.pallas_call(
        functools.partial(_recurrent_mixers_kernel, tt=tt),
        grid=(batch, nt),
        in_specs=[
            rows(LANES, FX_GATES),
            rows(dkh, MM_GLA_Q), rows(dkh, MM_GLA_K), rows(w, MM_GLA_V), rows(w, FX_GLA_R),
            pl.BlockSpec((LANES, dkh), const), pl.BlockSpec((1, dkh), const), pl.BlockSpec((1, w), const),
            rows(2 * w, FX_ML_QK), rows(w, MM_ML_V), rows(w, FX_ML_O),
            pl.BlockSpec((MLSTM_CONV, 2 * w), const), pl.BlockSpec((1, 2 * w), const),
            pl.BlockSpec((1, LANES), const), pl.BlockSpec((1, w), const),
        ],
        out_specs=[rows(w, 0), rows(w, 0)],
        out_shape=[out, out],
        scratch_shapes=[pltpu.VMEM((w, dkh), F32),
                        pltpu.VMEM((N_HEADS * HEAD_DV, HEAD_DV), F32),
                        pltpu.VMEM((N_HEADS, 1, HEAD_DV), F32),
                        pltpu.VMEM((N_HEADS, 1, 1), F32),
                        pltpu.VMEM((8, 2 * w), F32)],
        compiler_params=_params("parallel", "arbitrary"),
        name="recurrent_mixers",
    )(u_fx,
      u_mm, u_mm, u_mm, u_fx, wgu, gla_b_gate.reshape(1, dkh), gla_gain.reshape(1, w),
      u_fx, u_mm, u_fx, conv_w, conv_b.reshape(1, 2 * w), gate_bias, ml_gain.reshape(1, w))


BAND_Q_CHUNKS = 4
BAND_TQ = BAND_Q_CHUNKS * CHUNK
BAND_TK = (CA_LEFT_CHUNKS + BAND_Q_CHUNKS) * CHUNK
BAND_PAD = CA_LEFT_CHUNKS * CHUNK


def _band_bias_kernel(table_ref, o_ref):
    l = pl.program_id(0)
    h = pl.program_id(1)
    row = lax.broadcasted_iota(jnp.int32, (BAND_TQ, BAND_TK), 0)
    col = lax.broadcasted_iota(jnp.int32, (BAND_TQ, BAND_TK), 1)
    idx = jnp.clip(row - col + BAND_PAD, -REL_CLIP, REL_CLIP) + REL_CLIP
    delta = row // CHUNK + CA_LEFT_CHUNKS - col // CHUNK
    in_band = (delta >= 0) & (delta <= CA_LEFT_CHUNKS)

    table = table_ref[0, 0]
    tiles = []
    for c0 in range(0, BAND_TK, LANES):
        i = idx[:, c0:c0 + LANES]
        val = jnp.zeros(i.shape, F32)
        for ch in range(table.shape[0]):
            src = jnp.broadcast_to(table[ch:ch + 1, :], i.shape)
            looked_up = jnp.take_along_axis(src, i & (LANES - 1), axis=1)
            val = jnp.where(i // LANES == ch, looked_up, val)
        tiles.append(val)
    o_ref[0, 0] = jnp.where(in_band, jnp.concatenate(tiles, axis=1), NEG)


def band_bias(rel_bias):
    depth, heads, n = rel_bias.shape
    chunks = -(-n // LANES)
    table = jnp.pad(rel_bias, ((0, 0), (0, 0), (0, chunks * LANES - n))).reshape(depth, heads, chunks, LANES)
    return pl.pallas_call(
        _band_bias_kernel,
        grid=(depth, N_HEADS),
        in_specs=[pl.BlockSpec((1, 1, chunks, LANES), lambda l, h: (l, h, 0, 0))],
        out_specs=pl.BlockSpec((1, 1, BAND_TQ, BAND_TK), lambda l, h: (l, h, 0, 0)),
        out_shape=jax.ShapeDtypeStruct((depth, N_HEADS, BAND_TQ, BAND_TK), F32),
        compiler_params=_params("parallel", "parallel"),
        name="band_bias",
    )(table)


def _band_attn_kernel(bias_ref, q_ref, k_ref, v_ref, o_ref, kp_ref, vp_ref, *, seq):
    kp_ref[0:BAND_PAD, :] = jnp.zeros((BAND_PAD, HEAD_DV), BF16)
    vp_ref[0:BAND_PAD, :] = jnp.zeros((BAND_PAD, HEAD_DV), BF16)
    kp_ref[BAND_PAD:BAND_PAD + seq, :] = k_ref[...]
    vp_ref[BAND_PAD:BAND_PAD + seq, :] = v_ref[...]
    col = lax.broadcasted_iota(jnp.int32, (BAND_TQ, BAND_TK), 1)
    scale = HEAD_DV ** -0.5

    def tile(t, _):
        r0 = pl.multiple_of(t * BAND_TQ, BAND_TQ)
        q = q_ref[pl.ds(r0, BAND_TQ), :]
        kb = kp_ref[pl.ds(r0, BAND_TK), :]
        vb = vp_ref[pl.ds(r0, BAND_TK), :]
        s = _nt(q, kb) * scale + bias_ref[0, 0]
        s = jnp.where(col + r0 >= BAND_PAD, s, NEG)
        p = jnp.exp(s - jnp.max(s, axis=-1, keepdims=True))
        o = _mm(p.astype(BF16), vb) / jnp.sum(p, axis=-1, keepdims=True)
        o_ref[pl.ds(r0, BAND_TQ), :] = o.astype(o_ref.dtype)
        return 0

    lax.fori_loop(0, seq // BAND_TQ, tile, 0, unroll=4)


def band_attention(u_mm, bias, layer, batch, seq):
    cq, ck, cv = MM_CA_Q // LANES, MM_CA_K // LANES, MM_CA_V // LANES
    kern = functools.partial(_band_attn_kernel, seq=seq)
    return pl.pallas_call(
        kern,
        grid=(batch, N_HEADS),
        in_specs=[
            pl.BlockSpec((1, 1, BAND_TQ, BAND_TK), lambda b, h: (layer, h, 0, 0)),
            pl.BlockSpec((seq, LANES), lambda b, h: (b, cq + h)),
            pl.BlockSpec((seq, LANES), lambda b, h: (b, ck + h)),
            pl.BlockSpec((seq, LANES), lambda b, h: (b, cv + h)),
        ],
        out_specs=pl.BlockSpec((seq, HEAD_DV), lambda b, h: (b, h)),
        out_shape=jax.ShapeDtypeStruct((batch * seq, BRANCH_WIDTH), BF16),
        scratch_shapes=[pltpu.VMEM((BAND_PAD + seq, HEAD_DV), BF16),
                        pltpu.VMEM((BAND_PAD + seq, HEAD_DV), BF16)],
        compiler_params=_params("parallel", "parallel"),
        name="band_attention",
    )(bias, u_mm, u_mm, u_mm)


def _gated_merge_kernel(h_ref, ba_ref, bb_ref, bc_ref, bd_ref, wg_ref, bg_ref, wb_ref, o_ref):
    h = h_ref[...]
    merged = None
    for bi, br_ref in enumerate((ba_ref, bb_ref, bc_ref, bd_ref)):
        gate = _sigmoid(_mm(h, wg_ref[bi]) + bg_ref[bi])
        term = gate * _mm(br_ref[...], wb_ref[bi])
        merged = term if merged is None else merged + term
    o_ref[...] = merged.astype(o_ref.dtype)


def gated_merge(h, branches, w_gate, b_gate, w_branch, tm=1024, tn=512):
    m, d = h.shape
    w = BRANCH_WIDTH
    tm = min(tm, m)
    row_d = pl.BlockSpec((tm, d), lambda i, j: (i, 0))
    row_w = pl.BlockSpec((tm, w), lambda i, j: (i, 0))
    return pl.pallas_call(
        _gated_merge_kernel,
        grid=(m // tm, d // tn),
        in_specs=[
            row_d, row_w, row_w, row_w, row_w,
            pl.BlockSpec((4, d, tn), lambda i, j: (0, 0, j)),
            pl.BlockSpec((4, 1, tn), lambda i, j: (0, 0, j)),
            pl.BlockSpec((4, w, tn), lambda i, j: (0, 0, j)),
        ],
        out_specs=pl.BlockSpec((tm, tn), lambda i, j: (i, j)),
        out_shape=jax.ShapeDtypeStruct((m, d), BF16),
        compiler_params=_params("parallel", "arbitrary"),
        name="gated_merge",
    )(h, *branches, w_gate, b_gate.reshape(4, 1, d), w_branch)


def _out_proj_kernel(a_ref, x_ref, wo_ref, gpost_ref, gnext_ref, xo_ref, ho_ref):
    x_new = x_ref[...] + _rms(_mm(a_ref[...], wo_ref[...]), gpost_ref[...])
    xo_ref[...] = x_new
    ho_ref[...] = _rms(x_new, gnext_ref[...]).astype(ho_ref.dtype)


def out_proj(a, x, w_out, g_post, g_next, tm=512):
    m, d = x.shape
    row_d = pl.BlockSpec((tm, d), lambda i: (i, 0))
    vec_d = pl.BlockSpec((1, d), lambda i: (0, 0))
    return pl.pallas_call(
        _out_proj_kernel,
        grid=(m // tm,),
        in_specs=[row_d, row_d, pl.BlockSpec((d, d), lambda i: (0, 0)), vec_d, vec_d],
        out_specs=[row_d, row_d],
        out_shape=[jax.ShapeDtypeStruct((m, d), F32), jax.ShapeDtypeStruct((m, d), BF16)],
        compiler_params=_params("parallel"),
        name="out_proj",
    )(a, x, w_out, g_post.reshape(1, d), g_next.reshape(1, d))


def _ffn_kernel(h_ref, x_hbm, wu_ref, wd_ref, gpost_ref, *rest, emit_next, sub):
    if emit_next:
        gnext_ref, xo_ref, ho_ref, x_ref, x_sem = rest
    else:
        xo_ref, x_ref, x_sem = rest
    i = pl.program_id(0)
    j = pl.program_id(1)
    tm = x_ref.shape[0]

    def x_copy():
        return pltpu.make_async_copy(x_hbm.at[pl.ds(pl.multiple_of(i * tm, tm), tm)], x_ref, x_sem)

    @pl.when(j == 0)
    def _():
        x_copy().start()
        xo_ref[...] = jnp.zeros_like(xo_ref)

    for s in range(wu_ref.shape[1] // sub):
        cols = slice(s * sub, (s + 1) * sub)
        act = jnp.square(jnp.maximum(_mm(h_ref[...], wu_ref[:, cols]), 0.0))
        xo_ref[...] += _mm(act.astype(BF16), wd_ref[cols, :])

    @pl.when(j == pl.num_programs(1) - 1)
    def _():
        x_copy().wait()
        x_new = x_ref[...] + _rms(xo_ref[...], gpost_ref[...])
        xo_ref[...] = x_new
        if emit_next:
            ho_ref[...] = _rms(x_new, gnext_ref[...]).astype(ho_ref.dtype)


def ffn(h, x, w_up, w_down, g_post, g_next=None, tm=1024, tf=1024, sub=512):
    m, d = x.shape
    f = w_up.shape[1]
    tm = min(tm, m)
    emit_next = g_next is not None
    row_d = pl.BlockSpec((tm, d), lambda i, j: (i, 0))
    vec_d = pl.BlockSpec((1, d), lambda i, j: (0, 0))
    in_specs = [row_d, pl.BlockSpec(memory_space=pl.ANY), pl.BlockSpec((d, tf), lambda i, j: (0, j)),
                pl.BlockSpec((tf, d), lambda i, j: (j, 0)), vec_d]
    args = [h, x, w_up, w_down, g_post.reshape(1, d)]
    out_specs = [row_d]
    out_shape = [jax.ShapeDtypeStruct((m, d), F32)]
    if emit_next:
        in_specs.append(vec_d)
        args.append(g_next.reshape(1, d))
        out_specs.append(row_d)
        out_shape.append(jax.ShapeDtypeStruct((m, d), BF16))
    return pl.pallas_call(
        functools.partial(_ffn_kernel, emit_next=emit_next, sub=sub),
        grid=(m // tm, f // tf),
        in_specs=in_specs,
        out_specs=out_specs,
        out_shape=out_shape,
        scratch_shapes=[pltpu.VMEM((tm, d), F32), pltpu.SemaphoreType.DMA(())],
        compiler_params=pltpu.CompilerParams(dimension_semantics=("parallel", "arbitrary"),
                                             vmem_limit_bytes=FFN_VMEM_LIMIT),
        name="ffn",
    )(*args)


def _pack_w_in(w_in_l):
    w = BRANCH_WIDTH
    gla0 = 3 * w
    ml0 = gla0 + 2 * N_HEADS * GLA_DK + w + GLA_GATE_RANK + w
    ca0 = ml0 + 4 * w + 2 * N_HEADS
    c = lambda a, b: w_in_l[:, a:b]
    g_low0 = gla0 + 2 * N_HEADS * GLA_DK + w
    gates = jnp.concatenate([c(g_low0, g_low0 + GLA_GATE_RANK), c(ml0 + 4 * w, ml0 + 4 * w + 2 * N_HEADS)], axis=1)
    gates = jnp.pad(gates, ((0, 0), (0, LANES - gates.shape[1])))
    w_mm = jnp.concatenate([c(0, 3 * w), c(gla0, gla0 + 2 * N_HEADS * GLA_DK + w),
                            c(ml0 + 2 * w, ml0 + 3 * w), c(ca0, ca0 + 3 * w)], axis=1)
    w_fx = jnp.concatenate([c(ml0, ml0 + 2 * w), c(g_low0 + GLA_GATE_RANK, g_low0 + GLA_GATE_RANK + w),
                            c(ml0 + 3 * w, ml0 + 4 * w), gates], axis=1)
    return w_mm, w_fx


def kernel(x, norm_mix_pre, norm_mix_post, w_in, diff_lambda, diff_norm, gla_w_gate_up, gla_b_gate, gla_norm,
           mlstm_conv_w, mlstm_conv_b, mlstm_b_i, mlstm_b_f, mlstm_norm, rel_bias, w_branch, w_gate, b_gate,
           w_out, norm_ffn_pre, norm_ffn_post, w_up, w_down):
    batch, seq, d = x.shape
    depth = w_in.shape[0]
    m = batch * seq
    xf = x.reshape(m, d)
    bias_tiles = band_bias(rel_bias)
    h = rmsnorm_cast(xf, norm_mix_pre[0])
    for l in range(depth):
        w_mm, w_fx = (a.astype(BF16) for a in _pack_w_in(w_in[l]))
        u_mm = matmul(h, w_mm, BF16, tm=min(1024, m), tn=MM_COLS // 3, name="in_proj_mm")
        u_fx = matmul(h, w_fx, F32, tm=min(1024, m), tn=FX_COLS, name="in_proj_fx")
        o_gla, o_mlstm = recurrent_mixers(u_mm, u_fx, gla_w_gate_up[l], gla_b_gate[l], gla_norm[l], mlstm_conv_w[l],
                                          mlstm_conv_b[l], mlstm_b_i[l], mlstm_b_f[l], mlstm_norm[l], batch, seq)
        branches = (
            diff_attention(u_mm, diff_lambda[l], diff_norm[l], l, batch, seq),
            o_gla,
            o_mlstm,
            band_attention(u_mm, bias_tiles, l, batch, seq),
        )
        merged = gated_merge(h, branches, w_gate[l].astype(BF16), b_gate[l], w_branch[l].astype(BF16))
        xf, h2 = out_proj(merged, xf, w_out[l].astype(BF16), norm_mix_post[l], norm_ffn_pre[l])
        g_next = norm_mix_pre[l + 1] if l + 1 < depth else None
        outs = ffn(h2, xf, w_up[l].astype(BF16), w_down[l].astype(BF16), norm_ffn_post[l], g_next)
        xf, h = outs if g_next is not None else (outs[0], None)
    return xf.reshape(batch, seq, d)
```

```python
import functools
import math

import jax
import jax.numpy as jnp
from jax import lax
from jax.experimental import pallas as pl
from jax.experimental.pallas import tpu as pltpu

F32 = jnp.float32
BF16 = jnp.bfloat16

D_MODEL = 2048
CHUNK = 64
BRANCH_WIDTH = 512
N_HEADS = 4
HEAD_DV = 128
DIFF_QK_DIM = 64
GLA_DK = 64
GLA_GATE_RANK = 16
GLA_GATE_NORM = 16.0
MLSTM_CONV = 4
CA_LEFT_CHUNKS = 8
REL_CLIP = 2 * CHUNK
D_FF = 4 * D_MODEL
EPS = 1e-6
NEG = -1e30

LANES = 128
VMEM_LIMIT = 56 * 1024 * 1024
FFN_VMEM_LIMIT = 60 * 1024 * 1024

MM_DIFF_Q, MM_DIFF_K, MM_DIFF_V = 0, 512, 1024
MM_GLA_Q, MM_GLA_K, MM_GLA_V = 1536, 1792, 2048
MM_ML_V = 2560
MM_CA_Q, MM_CA_K, MM_CA_V = 3072, 3584, 4096
MM_COLS = 4608
FX_ML_QK, FX_GLA_R, FX_ML_O, FX_GATES = 0, 1024, 1536, 2048
FX_COLS = 2176
GATE_GLA_LOW, GATE_ML_I, GATE_ML_F = 0, 16, 20

SUM_ROWS = 16

NT_DIMS = (((1,), (1,)), ((), ()))
TN_DIMS = (((0,), (0,)), ((), ()))


def _nt(a, b):
    return lax.dot_general(a, b, NT_DIMS, preferred_element_type=F32)


def _tn(a, b):
    return lax.dot_general(a, b, TN_DIMS, preferred_element_type=F32)


def _mm(a, b):
    return jnp.dot(a, b, preferred_element_type=F32)


def _tri_cumsum(tril, x):
    hi = x.astype(BF16)
    rest = x - hi.astype(F32)
    mid = rest.astype(BF16)
    lo = (rest - mid.astype(F32)).astype(BF16)
    n = x.shape[1]
    parts = _mm(tril, jnp.concatenate([hi, mid, lo], axis=1))
    return parts[:, :n] + parts[:, n:2 * n] + parts[:, 2 * n:]


def _rms(x, gain):
    return x * lax.rsqrt(jnp.mean(x * x, axis=-1, keepdims=True) + EPS) * gain


def _log_sigmoid(x):
    return jnp.minimum(x, 0.0) - jnp.log1p(jnp.exp(-jnp.abs(x)))


def _sigmoid(x):
    return 0.5 * jnp.tanh(0.5 * x) + 0.5


def _params(*sem):
    return pltpu.CompilerParams(dimension_semantics=sem, vmem_limit_bytes=VMEM_LIMIT)


def _rmsnorm_cast_kernel(x_ref, g_ref, o_ref):
    o_ref[...] = _rms(x_ref[...], g_ref[...]).astype(o_ref.dtype)


def rmsnorm_cast(x, g, tm=512):
    m, d = x.shape
    return pl.pallas_call(
        _rmsnorm_cast_kernel,
        grid=(m // tm,),
        in_specs=[pl.BlockSpec((tm, d), lambda i: (i, 0)), pl.BlockSpec((1, d), lambda i: (0, 0))],
        out_specs=pl.BlockSpec((tm, d), lambda i: (i, 0)),
        out_shape=jax.ShapeDtypeStruct((m, d), BF16),
        compiler_params=_params("parallel"),
        name="rmsnorm_cast",
    )(x, g.reshape(1, d))


def _matmul_kernel(a_ref, b_ref, o_ref):
    o_ref[...] = _mm(a_ref[...], b_ref[...]).astype(o_ref.dtype)


def matmul(a, b, out_dtype, tm, tn, name):
    m, k = a.shape
    n = b.shape[1]
    return pl.pallas_call(
        _matmul_kernel,
        grid=(m // tm, n // tn),
        in_specs=[pl.BlockSpec((tm, k), lambda i, j: (i, 0)), pl.BlockSpec((k, tn), lambda i, j: (0, j))],
        out_specs=pl.BlockSpec((tm, tn), lambda i, j: (i, j)),
        out_shape=jax.ShapeDtypeStruct((m, n), out_dtype),
        compiler_params=_params("parallel", "arbitrary"),
        name=name,
    )(a, b)


def _diff_attn_kernel(slopes_ref, lam_ref, gain_ref, q_ref, k_ref, v_ref, o_ref,
                      vt_ref, kb_ref, diag_ref, m_ref, acc_ref, *, t, lam_init):
    h = pl.program_id(1)
    qi = pl.program_id(2)
    slope = slopes_ref[h]
    q0 = pl.multiple_of(qi * t, t)
    first_map =lax.broadcasted_iota(jnp.int32, (1, LANES), 1) < DIFF_QK_DIM

    @pl.when(qi == 0)
    def _():
        vt_ref[0:HEAD_DV, :] = v_ref[...].astype(F32).T.astype(BF16)
        vt_ref[HEAD_DV:, :] = jnp.ones((SUM_ROWS, v_ref.shape[0]), BF16)
        key = lax.broadcasted_iota(jnp.int32, (t, t), 0)
        qry = lax.broadcasted_iota(jnp.int32, (t, t), 1)
        past = slope * key.astype(F32)
        visible = key < (qry // CHUNK + 1) * CHUNK
        diag = jnp.where(visible, slope * (qry - jnp.abs(qry - key)).astype(F32), NEG)
        for mi in range(2):
            kb_ref[:, mi * t:(mi + 1) * t] = past
            diag_ref[:, mi * t:(mi + 1) * t] = diag

    q = q_ref[...] * jnp.asarray(DIFF_QK_DIM ** -0.5, BF16)
    q_maps = jnp.concatenate([jnp.where(first_map, q, jnp.zeros_like(q)),
                              jnp.where(first_map, jnp.zeros_like(q), q)], axis=0)
    m_ref[...] = jnp.full(m_ref.shape, NEG, F32)
    acc_ref[...] = jnp.zeros_like(acc_ref)

    def accumulate(z, shift, vt):
        m_prev = m_ref[...]
        m_new = jnp.maximum(m_prev, jnp.max(z, axis=0, keepdims=True) + shift)
        p = jnp.exp(z - (m_new - shift))
        acc_ref[...] = jnp.exp(m_prev - m_new) * acc_ref[...] + _mm(vt, p.astype(BF16))
        m_ref[...] = m_new

    def past_tile(j, _):
        k0 = pl.multiple_of(j * t, t)
        z = _nt(k_ref[pl.ds(k0, t), :], q_maps) + kb_ref[...]
        accumulate(z, slope * (k0 - q0).astype(F32), vt_ref[:, pl.ds(k0, t)])
        return 0

    def two_past_tiles(jj, _):
        past_tile(2 * jj, 0)
        past_tile(2 * jj + 1, 0)
        return 0

    lax.fori_loop(0, qi // 2, two_past_tiles, 0)

    @pl.when(qi % 2 == 1)
    def _():
        past_tile(qi - 1, 0)

    z = _nt(k_ref[pl.ds(q0, t), :], q_maps) + diag_ref[...]
    accumulate(z, 0.0, vt_ref[:, pl.ds(q0, t)])

    lam = lam_ref[...]
    lam_full = (jnp.exp(jnp.sum(lam[0:1] * lam[1:2], axis=-1, keepdims=True))
                - jnp.exp(jnp.sum(lam[2:3] * lam[3:4], axis=-1, keepdims=True)) + lam_init)
    acc = acc_ref[...]
    ratio = acc[:HEAD_DV] / acc[HEAD_DV:HEAD_DV + 1]
    o = (ratio[:, :t] - lam_full * ratio[:, t:]).T
    o_ref[...] = (_rms(o, gain_ref[...]) * (1.0 - lam_init)).astype(o_ref.dtype)


def diff_attention(u_mm, lam, gain, layer_idx, batch, seq, tq=512):
    lam_init = 0.8 - 0.6 * math.exp(-0.3 * layer_idx)
    nq = seq // tq
    slopes = jnp.asarray([2.0 ** (-8.0 * (i + 1) / N_HEADS) for i in range(N_HEADS)], F32)
    cq, ck, cv = MM_DIFF_Q // LANES, MM_DIFF_K // LANES, MM_DIFF_V // LANES
    kern = functools.partial(_diff_attn_kernel, t=tq, lam_init=lam_init)
    return pl.pallas_call(
        kern,
        grid=(batch, N_HEADS, nq),
        in_specs=[
            pl.BlockSpec(memory_space=pltpu.SMEM),
            pl.BlockSpec((4, DIFF_QK_DIM), lambda b, h, i: (0, 0)),
            pl.BlockSpec((1, HEAD_DV), lambda b, h, i: (0, h)),
            pl.BlockSpec((tq, LANES), lambda b, h, i: (b * nq + i, cq + h)),
            pl.BlockSpec((seq, LANES), lambda b, h, i: (b, ck + h)),
            pl.BlockSpec((seq, LANES), lambda b, h, i: (b, cv + h)),
        ],
        out_specs=pl.BlockSpec((tq, HEAD_DV), lambda b, h, i: (b * nq + i, h)),
        out_shape=jax.ShapeDtypeStruct((batch * seq, BRANCH_WIDTH), BF16),
        scratch_shapes=[pltpu.VMEM((HEAD_DV + SUM_ROWS, seq), BF16),
                        pltpu.VMEM((tq, 2 * tq), F32),
                        pltpu.VMEM((tq, 2 * tq), F32),
                        pltpu.VMEM((1, 2 * tq), F32),
                        pltpu.VMEM((HEAD_DV + SUM_ROWS, 2 * tq), F32)],
        compiler_params=_params("parallel", "parallel", "arbitrary"),
        name="diff_attention",
    )(slopes, lam, gain.reshape(1, BRANCH_WIDTH), u_mm, u_mm, u_mm)


def _recurrent_mixers_kernel(
        g_ref,
        gq_ref, gk_ref, gv_ref, gr_ref, wgu_ref, bg_ref, ggain_ref,
        qk_ref, v_ref, op_ref, cw_ref, cb_ref, gb_ref, gain_ref,
        go_ref, o_ref,
        st_ref, c_ref, n_ref, m_ref, halo_ref, *, tt):
    w = BRANCH_WIDTH
    halo = halo_ref.shape[0]

    @pl.when(pl.program_id(1) == 0)
    def _():
        st_ref[...] = jnp.zeros_like(st_ref)
        c_ref[...] = jnp.zeros_like(c_ref)
        n_ref[...] = jnp.zeros_like(n_ref)
        m_ref[...] = jnp.zeros_like(m_ref)
        halo_ref[...] = jnp.zeros_like(halo_ref)

    dkh = N_HEADS * GLA_DK
    hc = N_HEADS * CHUNK
    iota = lambda shape, dim: lax.broadcasted_iota(jnp.int32, shape, dim)
    tril = (iota((CHUNK, CHUNK), 0) >= iota((CHUNK, CHUNK), 1)).astype(BF16)
    lane = iota((1, LANES), 1)
    head_of_dk_lane = iota((1, dkh), 1) // GLA_DK
    head_of_dv_lane = iota((1, w), 1) // HEAD_DV
    gla_causal = iota((CHUNK, hc), 0) >= iota((CHUNK, hc), 1) % CHUNK
    gla_state_block = iota((w, dkh), 0) // HEAD_DV == iota((w, dkh), 1) // GLA_DK
    r_i, c_i = iota((hc, hc), 0), iota((hc, hc), 1)
    ml_visible = (r_i // CHUNK == c_i // CHUNK) & (r_i % CHUNK >= c_i % CHUNK)

    def per_head_rows(a, width):
        return jnp.concatenate([a[:, h * width:(h + 1) * width] for h in range(N_HEADS)], axis=0)

    def head_blocks(a, head_of_lane):
        return jnp.concatenate([jnp.where(head_of_lane == h, a, jnp.zeros_like(a)) for h in range(N_HEADS)], axis=0)

    def gla_chunk(c):
        r0 = pl.multiple_of(c * CHUNK, CHUNK)
        rows = pl.ds(r0, CHUNK)
        pre = _mm(g_ref[rows, :].astype(BF16), wgu_ref[...]) + bg_ref[...]
        log_a = _log_sigmoid(pre) * (1.0 / GLA_GATE_NORM)
        cum = _tri_cumsum(tril, log_a)
        total = cum[CHUNK - 1:CHUNK, :]
        qf = gq_ref[rows, :].astype(F32) * (GLA_DK ** -0.5)
        kf = gk_ref[rows, :].astype(F32)
        q_dec = (qf * jnp.exp(cum)).astype(BF16)
        k_inv = (kf * jnp.exp(-cum)).astype(BF16)
        k_tail = (kf * jnp.exp(total - cum)).astype(BF16)
        v = gv_ref[rows, :]
        att = jnp.where(gla_causal, _nt(q_dec, head_blocks(k_inv, head_of_dk_lane)), 0.0)
        st = st_ref[...]
        o = _mm(att.astype(BF16), head_blocks(v, head_of_dv_lane)) + _nt(q_dec, st.astype(BF16))
        st_ref[...] = jnp.exp(total) * st + jnp.where(gla_state_block, _tn(v, k_tail), 0.0)
        for h in range(N_HEADS):
            vs = slice(h * HEAD_DV, (h + 1) * HEAD_DV)
            r = gr_ref[rows, vs]
            y = _rms(o[:, vs], ggain_ref[:, vs]) * (r * _sigmoid(r))
            go_ref[rows, vs] = y.astype(go_ref.dtype)

    def mlstm_chunk(c):
        r0 = pl.multiple_of(c * CHUNK, CHUNK)
        rows = pl.ds(r0, CHUNK)
        p0 = pl.multiple_of(jnp.maximum(r0 - halo, 0), halo)
        prev = jnp.where(c > 0, qk_ref[pl.ds(p0, halo), :], halo_ref[...])
        ext = jnp.concatenate([prev, qk_ref[rows, :]], axis=0)
        acc = cb_ref[...]
        for j in range(MLSTM_CONV):
            s0 = halo - (MLSTM_CONV - 1) + j
            acc = acc + cw_ref[j:j + 1, :] * ext[s0:s0 + CHUNK, :]
        qk = acc * _sigmoid(acc)
        gb = g_ref[rows, :] + gb_ref[...]
        log_f = _log_sigmoid(gb)
        bcum = _tri_cumsum(tril, log_f)
        x = jnp.where(lane < GATE_ML_F, gb, bcum)
        xt = x.T
        heads = range(N_HEADS)
        rep = lambda per_head: jnp.concatenate(
            [jnp.broadcast_to(a, (CHUNK, a.shape[-1])) for a in per_head], axis=0)
        bcum_c = jnp.concatenate([bcum[:, GATE_ML_F + h:GATE_ML_F + h + 1] for h in heads], axis=0)
        logi_c = jnp.concatenate([gb[:, GATE_ML_I + h:GATE_ML_I + h + 1] for h in heads], axis=0)
        r_row = jnp.concatenate([xt[GATE_ML_I + h:GATE_ML_I + h + 1, :] - xt[GATE_ML_F + h:GATE_ML_F + h + 1, :]
                                 for h in heads], axis=1)
        btot = [bcum[CHUNK - 1:CHUNK, GATE_ML_F + h:GATE_ML_F + h + 1] for h in heads]
        m_prev = [m_ref[h] for h in heads]
        tail = rep(btot) - bcum_c + logi_c
        m_loc = [jnp.max(tail[h * CHUNK:(h + 1) * CHUNK], axis=0, keepdims=True) for h in heads]
        w_tail = jnp.exp(tail - rep(m_loc))
        q_lanes = qk[:, :w].astype(BF16)
        q = per_head_rows(q_lanes, HEAD_DV)
        kf = per_head_rows(qk[:, w:], HEAD_DV) * (HEAD_DV ** -0.5)
        v = per_head_rows(v_ref[rows, :], HEAD_DV)
        kw = kf * w_tail
        dlog = jnp.where(ml_visible, bcum_c + r_row, NEG)
        inter_log = bcum_c + rep(m_prev)
        m_t = jnp.maximum(jnp.max(dlog, axis=-1, keepdims=True), inter_log)
        a_mat = jnp.exp(dlog - m_t) * _nt(q, kf.astype(BF16))
        inter_w = jnp.exp(inter_log - m_t)
        c_prev = c_ref[...]
        n_prev = [n_ref[h] for h in heads]
        num = _mm(a_mat.astype(BF16), v) + inter_w * _mm(head_blocks(q_lanes, head_of_dv_lane), c_prev.astype(BF16))
        qn = jnp.sum(q.astype(F32) * rep(n_prev), axis=-1, keepdims=True)
        den = jnp.sum(a_mat, axis=-1, keepdims=True) + inter_w * qn
        hh = num / jnp.maximum(jnp.abs(den), jnp.exp(-m_t))
        for h in heads:
            hs = slice(h * HEAD_DV, (h + 1) * HEAD_DV)
            y = _rms(_sigmoid(op_ref[rows, hs]) * hh[h * CHUNK:(h + 1) * CHUNK], gain_ref[:, hs])
            o_ref[rows, hs] = y.astype(o_ref.dtype)
        kw_lanes = jnp.concatenate([kw[h * CHUNK:(h + 1) * CHUNK] for h in heads], axis=1)
        d_c = _tn(head_blocks(kw_lanes.astype(BF16), head_of_dv_lane), v)
        rep_d = lambda per_head: jnp.concatenate(
            [jnp.broadcast_to(a, (HEAD_DV, 1)) for a in per_head], axis=0)
        m_new = [jnp.maximum(btot[h] + m_prev[h], m_loc[h]) for h in heads]
        a = [jnp.exp(btot[h] + m_prev[h] - m_new[h]) for h in heads]
        bb = [jnp.exp(m_loc[h] - m_new[h]) for h in heads]
        c_ref[...] = rep_d(a) * c_prev + rep_d(bb) * d_c
        for h in heads:
            d_n = jnp.sum(kw[h * CHUNK:(h + 1) * CHUNK], axis=0, keepdims=True)
            n_ref[h] = a[h] * n_prev[h] + bb[h] * d_n
            m_ref[h] = m_new[h]

    def chunk(c, _):
        gla_chunk(c)
        mlstm_chunk(c)
        return 0

    lax.fori_loop(0, tt // CHUNK, chunk, 0, unroll=4)
    halo_ref[...] = qk_ref[tt - halo:tt, :]


def recurrent_mixers(u_mm, u_fx, gla_w_gate_up, gla_b_gate, gla_gain, conv_w, conv_b, b_i, b_f, ml_gain,
                     batch, seq, tt=1024):
    w = BRANCH_WIDTH
    dkh = N_HEADS * GLA_DK
    tt = min(tt, seq)
    nt = seq // tt
    wgu = (jnp.zeros((LANES, dkh), F32).at[GATE_GLA_LOW:GATE_GLA_LOW + GLA_GATE_RANK].set(gla_w_gate_up)
           .astype(BF16))
    gate_bias = (jnp.zeros((1, LANES), F32).at[0, GATE_ML_I:GATE_ML_I + N_HEADS].set(b_i)
                 .at[0, GATE_ML_F:GATE_ML_F + N_HEADS].set(b_f))
    const = lambda b, t: (0, 0)
    rows = lambda width, col0: pl.BlockSpec((tt, width), lambda b, t: (b * nt + t, col0 // width))
    out = jax.ShapeDtypeStruct((batch * seq, w), BF16)
    return pl.pallas_call(
        functools.partial(_recurrent_mixers_kernel, tt=tt),
        grid=(batch, nt),
        in_specs=[
            rows(LANES, FX_GATES),
            rows(dkh, MM_GLA_Q), rows(dkh, MM_GLA_K), rows(w, MM_GLA_V), rows(w, FX_GLA_R),
            pl.BlockSpec((LANES, dkh), const), pl.BlockSpec((1, dkh), const), pl.BlockSpec((1, w), const),
            rows(2 * w, FX_ML_QK), rows(w, MM_ML_V), rows(w, FX_ML_O),
            pl.BlockSpec((MLSTM_CONV, 2 * w), const), pl.BlockSpec((1, 2 * w), const),
            pl.BlockSpec((1, LANES), const), pl.BlockSpec((1, w), const),
        ],
        out_specs=[rows(w, 0), rows(w, 0)],
        out_shape=[out, out],
        scratch_shapes=[pltpu.VMEM((w, dkh), F32),
                        pltpu.VMEM((N_HEADS * HEAD_DV, HEAD_DV), F32),
                        pltpu.VMEM((N_HEADS, 1, HEAD_DV), F32),
                        pltpu.VMEM((N_HEADS, 1, 1), F32),
                        pltpu.VMEM((8, 2 * w), F32)],
        compiler_params=_params("parallel", "arbitrary"),
        name="recurrent_mixers",
    )(u_fx,
      u_mm, u_mm, u_mm, u_fx, wgu, gla_b_gate.reshape(1, dkh), gla_gain.reshape(1, w),
      u_fx, u_mm, u_fx, conv_w, conv_b.reshape(1, 2 * w), gate_bias, ml_gain.reshape(1, w))


BAND_Q_CHUNKS = 4
BAND_TQ = BAND_Q_CHUNKS * CHUNK
BAND_TK = (CA_LEFT_CHUNKS + BAND_Q_CHUNKS) * CHUNK
BAND_PAD = CA_LEFT_CHUNKS * CHUNK


def _band_bias_kernel(table_ref, o_ref):
    l = pl.program_id(0)
    h = pl.program_id(1)
    row = lax.broadcasted_iota(jnp.int32, (BAND_TQ, BAND_TK), 0)
    col = lax.broadcasted_iota(jnp.int32, (BAND_TQ, BAND_TK), 1)
    idx = jnp.clip(row - col + BAND_PAD, -REL_CLIP, REL_CLIP) + REL_CLIP
    delta = row // CHUNK + CA_LEFT_CHUNKS - col // CHUNK
    in_band = (delta >= 0) & (delta <= CA_LEFT_CHUNKS)

    table = table_ref[0, 0]
    tiles = []
    for c0 in range(0, BAND_TK, LANES):
        i = idx[:, c0:c0 + LANES]
        val = jnp.zeros(i.shape, F32)
        for ch in range(table.shape[0]):
            src = jnp.broadcast_to(table[ch:ch + 1, :], i.shape)
            looked_up = jnp.take_along_axis(src, i & (LANES - 1), axis=1)
            val = jnp.where(i // LANES == ch, looked_up, val)
        tiles.append(val)
    o_ref[0, 0] = jnp.where(in_band, jnp.concatenate(tiles, axis=1), NEG)


def band_bias(rel_bias):
    depth, heads, n = rel_bias.shape
    chunks = -(-n // LANES)
    table = jnp.pad(rel_bias, ((0, 0), (0, 0), (0, chunks * LANES - n))).reshape(depth, heads, chunks, LANES)
    return pl.pallas_call(
        _band_bias_kernel,
        grid=(depth, N_HEADS),
        in_specs=[pl.BlockSpec((1, 1, chunks, LANES), lambda l, h: (l, h, 0, 0))],
        out_specs=pl.BlockSpec((1, 1, BAND_TQ, BAND_TK), lambda l, h: (l, h, 0, 0)),
        out_shape=jax.ShapeDtypeStruct((depth, N_HEADS, BAND_TQ, BAND_TK), F32),
        compiler_params=_params("parallel", "parallel"),
        name="band_bias",
    )(table)


def _band_attn_kernel(bias_ref, q_ref, k_ref, v_ref, o_ref, kp_ref, vp_ref, *, seq):
    kp_ref[0:BAND_PAD, :] = jnp.zeros((BAND_PAD, HEAD_DV), BF16)
    vp_ref[0:BAND_PAD, 0:HEAD_DV] = jnp.zeros((BAND_PAD, HEAD_DV), BF16)
    kp_ref[BAND_PAD:BAND_PAD + seq, :] = k_ref[...]
    vp_ref[BAND_PAD:BAND_PAD + seq, 0:HEAD_DV] = v_ref[...]
    vp_ref[:, HEAD_DV:] = jnp.ones((BAND_PAD + seq, HEAD_DV), BF16)
    scale = HEAD_DV ** -0.5

    def tile(t, starts_in_pad):
        r0 = t * BAND_TQ
        r0 = r0 if starts_in_pad else pl.multiple_of(r0, BAND_TQ)
        q = q_ref[pl.ds(r0, BAND_TQ), :]
        kb = kp_ref[pl.ds(r0, BAND_TK), :]
        vb = vp_ref[pl.ds(r0, BAND_TK), :]
        s = _nt(q, kb) * scale + bias_ref[0, 0]
        if starts_in_pad:
            col = lax.broadcasted_iota(jnp.int32, (BAND_TQ, BAND_TK), 1)
            s = jnp.where(col + r0 >= BAND_PAD, s, NEG)
        p = jnp.exp(s - jnp.max(s, axis=-1, keepdims=True))
        ov = _mm(p.astype(BF16), vb)
        o_ref[pl.ds(r0, BAND_TQ), :] = (ov[:, :HEAD_DV] / ov[:, HEAD_DV:]).astype(o_ref.dtype)

    n_pad_tiles = BAND_PAD // BAND_TQ
    for t in range(n_pad_tiles):
        tile(t, True)

    def body(t, _):
        tile(t, False)
        return 0

    lax.fori_loop(n_pad_tiles, seq // BAND_TQ, body, 0, unroll=7)


def band_attention(u_mm, bias, layer, batch, seq):
    cq, ck, cv = MM_CA_Q // LANES, MM_CA_K // LANES, MM_CA_V // LANES
    kern = functools.partial(_band_attn_kernel, seq=seq)
    return pl.pallas_call(
        kern,
        grid=(batch, N_HEADS),
        in_specs=[
            pl.BlockSpec((1, 1, BAND_TQ, BAND_TK), lambda b, h: (layer, h, 0, 0)),
            pl.BlockSpec((seq, LANES), lambda b, h: (b, cq + h)),
            pl.BlockSpec((seq, LANES), lambda b, h: (b, ck + h)),
            pl.BlockSpec((seq, LANES), lambda b, h: (b, cv + h)),
        ],
        out_specs=pl.BlockSpec((seq, HEAD_DV), lambda b, h: (b, h)),
        out_shape=jax.ShapeDtypeStruct((batch * seq, BRANCH_WIDTH), BF16),
        scratch_shapes=[pltpu.VMEM((BAND_PAD + seq, HEAD_DV), BF16),
                        pltpu.VMEM((BAND_PAD + seq, 2 * HEAD_DV), BF16)],
        compiler_params=_params("parallel", "parallel"),
        name="band_attention",
    )(bias, u_mm, u_mm, u_mm)


def _gated_merge_kernel(h_ref, ba_ref, bb_ref, bc_ref, bd_ref, wg_ref, bg_ref, wb_ref, o_ref):
    h = h_ref[...]
    merged = None
    for bi, br_ref in enumerate((ba_ref, bb_ref, bc_ref, bd_ref)):
        gate = _sigmoid(_mm(h, wg_ref[bi]) + bg_ref[bi])
        term = gate * _mm(br_ref[...], wb_ref[bi])
        merged = term if merged is None else merged + term
    o_ref[...] = merged.astype(o_ref.dtype)


def gated_merge(h, branches, w_gate, b_gate, w_branch, tm=1024, tn=512):
    m, d = h.shape
    w = BRANCH_WIDTH
    tm = min(tm, m)
    row_d = pl.BlockSpec((tm, d), lambda i, j: (i, 0))
    row_w = pl.BlockSpec((tm, w), lambda i, j: (i, 0))
    return pl.pallas_call(
        _gated_merge_kernel,
        grid=(m // tm, d // tn),
        in_specs=[
            row_d, row_w, row_w, row_w, row_w,
            pl.BlockSpec((4, d, tn), lambda i, j: (0, 0, j)),
            pl.BlockSpec((4, 1, tn), lambda i, j: (0, 0, j)),
            pl.BlockSpec((4, w, tn), lambda i, j: (0, 0, j)),
        ],
        out_specs=pl.BlockSpec((tm, tn), lambda i, j: (i, j)),
        out_shape=jax.ShapeDtypeStruct((m, d), BF16),
        compiler_params=_params("parallel", "arbitrary"),
        name="gated_merge",
    )(h, *branches, w_gate, b_gate.reshape(4, 1, d), w_branch)


def _out_proj_kernel(a_ref, x_ref, wo_ref, gpost_ref, gnext_ref, xo_ref, ho_ref, *, sub):
    for r in range(a_ref.shape[0] // sub):
        rows = slice(r * sub, (r + 1) * sub)
        x_new = x_ref[rows, :] + _rms(_mm(a_ref[rows, :], wo_ref[...]), gpost_ref[...])
        xo_ref[rows, :] = x_new
        ho_ref[rows, :] = _rms(x_new, gnext_ref[...]).astype(ho_ref.dtype)


def out_proj(a, x, w_out, g_post, g_next, tm=512, sub=128):
    m, d = x.shape
    row_d = pl.BlockSpec((tm, d), lambda i: (i, 0))
    vec_d = pl.BlockSpec((1, d), lambda i: (0, 0))
    return pl.pallas_call(
        functools.partial(_out_proj_kernel, sub=min(sub, tm)),
        grid=(m // tm,),
        in_specs=[row_d, row_d, pl.BlockSpec((d, d), lambda i: (0, 0)), vec_d, vec_d],
        out_specs=[row_d, row_d],
        out_shape=[jax.ShapeDtypeStruct((m, d), F32), jax.ShapeDtypeStruct((m, d), BF16)],
        compiler_params=_params("parallel"),
        name="out_proj",
    )(a, x, w_out, g_post.reshape(1, d), g_next.reshape(1, d))


def _ffn_kernel(h_ref, x_hbm, wu_ref, wd_ref, gpost_ref, *rest, emit_next, sub):
    if emit_next:
        gnext_ref, xo_ref, ho_ref, x_ref, x_sem = rest
    else:
        xo_ref, x_ref, x_sem = rest
    i = pl.program_id(0)
    j = pl.program_id(1)
    tm = x_ref.shape[0]

    def x_copy():
        return pltpu.make_async_copy(x_hbm.at[pl.ds(pl.multiple_of(i * tm, tm), tm)], x_ref, x_sem)

    @pl.when(j == 0)
    def _():
        x_copy().start()
        xo_ref[...] = jnp.zeros_like(xo_ref)

    for s in range(wu_ref.shape[1] // sub):
        cols = slice(s * sub, (s + 1) * sub)
        act = jnp.square(jnp.maximum(_mm(h_ref[...], wu_ref[:, cols]), 0.0))
        xo_ref[...] += _mm(act.astype(BF16), wd_ref[cols, :])

    @pl.when(j == pl.num_programs(1) - 1)
    def _():
        x_copy().wait()
        x_new = x_ref[...] + _rms(xo_ref[...], gpost_ref[...])
        xo_ref[...] = x_new
        if emit_next:
            ho_ref[...] = _rms(x_new, gnext_ref[...]).astype(ho_ref.dtype)


def ffn(h, x, w_up, w_down, g_post, g_next=None, tm=1024, tf=1024, sub=512):
    m, d = x.shape
    f = w_up.shape[1]
    tm = min(tm, m)
    emit_next = g_next is not None
    row_d = pl.BlockSpec((tm, d), lambda i, j: (i, 0))
    vec_d = pl.BlockSpec((1, d), lambda i, j: (0, 0))
    in_specs = [row_d, pl.BlockSpec(memory_space=pl.ANY), pl.BlockSpec((d, tf), lambda i, j: (0, j)),
                pl.BlockSpec((tf, d), lambda i, j: (j, 0)), vec_d]
    args = [h, x, w_up, w_down, g_post.reshape(1, d)]
    out_specs = [row_d]
    out_shape = [jax.ShapeDtypeStruct((m, d), F32)]
    if emit_next:
        in_specs.append(vec_d)
        args.append(g_next.reshape(1, d))
        out_specs.append(row_d)
        out_shape.append(jax.ShapeDtypeStruct((m, d), BF16))
    return pl.pallas_call(
        functools.partial(_ffn_kernel, emit_next=emit_next, sub=sub),
        grid=(m // tm, f // tf),
        in_specs=in_specs,
        out_specs=out_specs,
        out_shape=out_shape,
        scratch_shapes=[pltpu.VMEM((tm, d), F32), pltpu.SemaphoreType.DMA(())],
        compiler_params=pltpu.CompilerParams(dimension_semantics=("parallel", "arbitrary"),
                                             vmem_limit_bytes=FFN_VMEM_LIMIT),
        name="ffn",
    )(*args)


def _pack_w_in(w_in_l):
    w = BRANCH_WIDTH
    gla0 = 3 * w
    ml0 = gla0 + 2 * N_HEADS * GLA_DK + w + GLA_GATE_RANK + w
    ca0 = ml0 + 4 * w + 2 * N_HEADS
    c = lambda a, b: w_in_l[:, a:b]
    g_low0 = gla0 + 2 * N_HEADS * GLA_DK + w
    gates = jnp.concatenate([c(g_low0, g_low0 + GLA_GATE_RANK), c(ml0 + 4 * w, ml0 + 4 * w + 2 * N_HEADS)], axis=1)
    gates = jnp.pad(gates, ((0, 0), (0, LANES - gates.shape[1])))
    w_mm = jnp.concatenate([c(0, 3 * w), c(gla0, gla0 + 2 * N_HEADS * GLA_DK + w),
                            c(ml0 + 2 * w, ml0 + 3 * w), c(ca0, ca0 + 3 * w)], axis=1)
    w_fx = jnp.concatenate([c(ml0, ml0 + 2 * w), c(g_low0 + GLA_GATE_RANK, g_low0 + GLA_GATE_RANK + w),
                            c(ml0 + 3 * w, ml0 + 4 * w), gates], axis=1)
    return w_mm, w_fx


def kernel(x, norm_mix_pre, norm_mix_post, w_in, diff_lambda, diff_norm, gla_w_gate_up, gla_b_gate, gla_norm,
           mlstm_conv_w, mlstm_conv_b, mlstm_b_i, mlstm_b_f, mlstm_norm, rel_bias, w_branch, w_gate, b_gate,
           w_out, norm_ffn_pre, norm_ffn_post, w_up, w_down):
    batch, seq, d = x.shape
    depth = w_in.shape[0]
    m = batch * seq
    xf = x.reshape(m, d)
    bias_tiles = band_bias(rel_bias)
    h = rmsnorm_cast(xf, norm_mix_pre[0])
    for l in range(depth):
        w_mm, w_fx = _pack_w_in(w_in[l].astype(BF16))
        u_mm = matmul(h, w_mm, BF16, tm=min(1024, m), tn=MM_COLS // 3, name="in_proj_mm")
        u_fx = matmul(h, w_fx, F32, tm=min(1024, m), tn=FX_COLS, name="in_proj_fx")
        o_gla, o_mlstm = recurrent_mixers(u_mm, u_fx, gla_w_gate_up[l], gla_b_gate[l], gla_norm[l], mlstm_conv_w[l],
                                          mlstm_conv_b[l], mlstm_b_i[l], mlstm_b_f[l], mlstm_norm[l], batch, seq)
        branches = (
            diff_attention(u_mm, diff_lambda[l], diff_norm[l], l, batch, seq),
            o_gla,
            o_mlstm,
            band_attention(u_mm, bias_tiles, l, batch, seq),
        )
        merged = gated_merge(h, branches, w_gate[l].astype(BF16), b_gate[l], w_branch[l].astype(BF16))
        xf, h2 = out_proj(merged, xf, w_out[l].astype(BF16), norm_mix_post[l], norm_ffn_pre[l])
        g_next = norm_mix_pre[l + 1] if l + 1 < depth else None
        outs = ffn(h2, xf, w_up[l].astype(BF16), w_down[l].astype(BF16), norm_ffn_post[l], g_next)
        xf, h = outs if g_next is not None else (outs[0], None)
    return xf.reshape(batch, seq, d)
```

```python
import functools
import math

import jax
import jax.numpy as jnp
from jax import lax
from jax.experimental import pallas as pl
from jax.experimental.pallas import tpu as pltpu

F32 = jnp.float32
BF16 = jnp.bfloat16

CHUNK = 64
BRANCH_WIDTH = 512
N_HEADS = 4
HEAD_DV = 128
DIFF_QK_DIM = 64
GLA_DK = 64
GLA_GATE_RANK = 16
GLA_GATE_NORM = 16.0
MLSTM_CONV = 4
CA_LEFT_CHUNKS = 8
REL_CLIP = 2 * CHUNK
EPS = 1e-6
NEG = -1e30

LANES = 128
VMEM_LIMIT = 56 * 1024 * 1024
FFN_VMEM_LIMIT = 60 * 1024 * 1024

MM_DIFF_Q, MM_DIFF_K, MM_DIFF_V = 0, 512, 1024
MM_GLA_Q, MM_GLA_K, MM_GLA_V = 1536, 1792, 2048
MM_ML_V = 2560
MM_CA_Q, MM_CA_K, MM_CA_V = 3072, 3584, 4096
MM_COLS = 4608
FX_ML_QK, FX_GLA_R, FX_ML_O, FX_GATES = 0, 1024, 1536, 2048
FX_COLS = 2176
GATE_GLA_LOW, GATE_ML_I, GATE_ML_F = 0, 16, 20

SUM_ROWS = 16

NT_DIMS = (((1,), (1,)), ((), ()))
TN_DIMS = (((0,), (0,)), ((), ()))


def _nt(a, b):
    return lax.dot_general(a, b, NT_DIMS, preferred_element_type=F32)


def _tn(a, b):
    return lax.dot_general(a, b, TN_DIMS, preferred_element_type=F32)


def _mm(a, b):
    return jnp.dot(a, b, preferred_element_type=F32)


def _tri_cumsum(tril, x):
    hi = x.astype(BF16)
    rest = x - hi.astype(F32)
    mid = rest.astype(BF16)
    lo = (rest - mid.astype(F32)).astype(BF16)
    n = x.shape[1]
    parts = _mm(tril, jnp.concatenate([hi, mid, lo], axis=1))
    return parts[:, :n] + parts[:, n:2 * n] + parts[:, 2 * n:]


def _rms(x, gain):
    return x * lax.rsqrt(jnp.mean(x * x, axis=-1, keepdims=True) + EPS) * gain


def _log_sigmoid(x):
    return jnp.minimum(x, 0.0) - jnp.log1p(jnp.exp(-jnp.abs(x)))


def _sigmoid(x):
    return 0.5 * jnp.tanh(0.5 * x) + 0.5


def _params(*sem):
    return pltpu.CompilerParams(dimension_semantics=sem, vmem_limit_bytes=VMEM_LIMIT)


def _rmsnorm_cast_kernel(x_ref, g_ref, o_ref):
    o_ref[...] = _rms(x_ref[...], g_ref[...]).astype(o_ref.dtype)


def rmsnorm_cast(x, g, tm=512):
    m, d = x.shape
    return pl.pallas_call(
        _rmsnorm_cast_kernel,
        grid=(m // tm,),
        in_specs=[pl.BlockSpec((tm, d), lambda i: (i, 0)), pl.BlockSpec((1, d), lambda i: (0, 0))],
        out_specs=pl.BlockSpec((tm, d), lambda i: (i, 0)),
        out_shape=jax.ShapeDtypeStruct((m, d), BF16),
        compiler_params=_params("parallel"),
        name="rmsnorm_cast",
    )(x, g.reshape(1, d))


def _matmul_kernel(a_ref, b_ref, o_ref):
    o_ref[...] = _mm(a_ref[...], b_ref[...]).astype(o_ref.dtype)


def matmul(a, b, out_dtype, tm, tn, name):
    m, k = a.shape
    n = b.shape[1]
    return pl.pallas_call(
        _matmul_kernel,
        grid=(m // tm, n // tn),
        in_specs=[pl.BlockSpec((tm, k), lambda i, j: (i, 0)), pl.BlockSpec((k, tn), lambda i, j: (0, j))],
        out_specs=pl.BlockSpec((tm, tn), lambda i, j: (i, j)),
        out_shape=jax.ShapeDtypeStruct((m, n), out_dtype),
        compiler_params=_params("parallel", "arbitrary"),
        name=name,
    )(a, b)


def _diff_attn_kernel(slopes_ref, lam_ref, gain_ref, q_ref, k_ref, v_ref, o_ref,
                      vt_ref, kb_ref, diag_ref, m_ref, acc_ref, *, t, lam_init):
    h = pl.program_id(1)
    step = pl.program_id(2)
    slope = slopes_ref[h]
    first_map = lax.broadcasted_iota(jnp.int32, (1, LANES), 1) < DIFF_QK_DIM

    @pl.when(step == 0)
    def _():
        vt_ref[0:HEAD_DV, :] = v_ref[...].astype(F32).T.astype(BF16)
        vt_ref[HEAD_DV:, :] = jnp.ones((SUM_ROWS, v_ref.shape[0]), BF16)
        key = lax.broadcasted_iota(jnp.int32, (t, t), 0)
        qry = lax.broadcasted_iota(jnp.int32, (t, t), 1)
        past = slope * key.astype(F32)
        visible = key < (qry // CHUNK + 1) * CHUNK
        diag = jnp.where(visible, slope * (qry - jnp.abs(qry - key)).astype(F32), NEG)
        for mi in range(2):
            kb_ref[:, mi * t:(mi + 1) * t] = past
            diag_ref[:, mi * t:(mi + 1) * t] = diag

    lam = lam_ref[...]
    lam_full = (jnp.exp(jnp.sum(lam[0:1] * lam[1:2], axis=-1, keepdims=True))
                - jnp.exp(jnp.sum(lam[2:3] * lam[3:4], axis=-1, keepdims=True)) + lam_init)

    def accumulate(z, shift, vt):
        m_prev = m_ref[...]
        m_new = jnp.maximum(m_prev, jnp.max(z, axis=0, keepdims=True) + shift)
        p = jnp.exp(z - (m_new - shift))
        acc_ref[...] = jnp.exp(m_prev - m_new) * acc_ref[...] + _mm(vt, p.astype(BF16))
        m_ref[...] = m_new

    def query_tile(qi, rows):
        q0 = pl.multiple_of(qi * t, t)
        q = q_ref[rows, :] * jnp.asarray(DIFF_QK_DIM ** -0.5, BF16)
        q_maps = jnp.concatenate([jnp.where(first_map, q, jnp.zeros_like(q)),
                                  jnp.where(first_map, jnp.zeros_like(q), q)], axis=0)
        m_ref[...] = jnp.full(m_ref.shape, NEG, F32)
        acc_ref[...] = jnp.zeros_like(acc_ref)

        def past_tile(j, _):
            k0 = pl.multiple_of(j * t, t)
            z = _nt(k_ref[pl.ds(k0, t), :], q_maps) + kb_ref[...]
            accumulate(z, slope * (k0 - q0).astype(F32), vt_ref[:, pl.ds(k0, t)])
            return 0

        def two_past_tiles(jj, _):
            past_tile(2 * jj, 0)
            past_tile(2 * jj + 1, 0)
            return 0

        lax.fori_loop(0, qi // 2, two_past_tiles, 0)

        @pl.when(qi % 2 == 1)
        def _():
            past_tile(qi - 1, 0)

        z = _nt(k_ref[pl.ds(q0, t), :], q_maps) + diag_ref[...]
        accumulate(z, 0.0, vt_ref[:, pl.ds(q0, t)])
        acc = acc_ref[...]
        ratio = acc[:HEAD_DV] / acc[HEAD_DV:HEAD_DV + 1]
        o = (ratio[:, :t] - lam_full * ratio[:, t:]).T
        o_ref[rows, :] = (_rms(o, gain_ref[...]) * (1.0 - lam_init)).astype(o_ref.dtype)

    tiles = q_ref.shape[0] // t
    for sub in range(tiles):
        query_tile(step * tiles + sub, slice(sub * t, (sub + 1) * t))


def diff_attention(u_mm, lam, gain, layer_idx, batch, seq, tq=512, tiles_per_step=2):
    lam_init = 0.8 - 0.6 * math.exp(-0.3 * layer_idx)
    tiles_per_step = min(tiles_per_step, seq // tq)
    rows = tq * tiles_per_step
    nq = seq // rows
    slopes = jnp.asarray([2.0 ** (-8.0 * (i + 1) / N_HEADS) for i in range(N_HEADS)], F32)
    cq, ck, cv = MM_DIFF_Q // LANES, MM_DIFF_K // LANES, MM_DIFF_V // LANES
    kern = functools.partial(_diff_attn_kernel, t=tq, lam_init=lam_init)
    return pl.pallas_call(
        kern,
        grid=(batch, N_HEADS, nq),
        in_specs=[
            pl.BlockSpec(memory_space=pltpu.SMEM),
            pl.BlockSpec((4, DIFF_QK_DIM), lambda b, h, i: (0, 0)),
            pl.BlockSpec((1, HEAD_DV), lambda b, h, i: (0, h)),
            pl.BlockSpec((rows, LANES), lambda b, h, i: (b * nq + i, cq + h)),
            pl.BlockSpec((seq, LANES), lambda b, h, i: (b, ck + h)),
            pl.BlockSpec((seq, LANES), lambda b, h, i: (b, cv + h)),
        ],
        out_specs=pl.BlockSpec((rows, HEAD_DV), lambda b, h, i: (b * nq + i, h)),
        out_shape=jax.ShapeDtypeStruct((batch * seq, BRANCH_WIDTH), BF16),
        scratch_shapes=[pltpu.VMEM((HEAD_DV + SUM_ROWS, seq), BF16),
                        pltpu.VMEM((tq, 2 * tq), F32),
                        pltpu.VMEM((tq, 2 * tq), F32),
                        pltpu.VMEM((1, 2 * tq), F32),
                        pltpu.VMEM((HEAD_DV + SUM_ROWS, 2 * tq), F32)],
        compiler_params=_params("parallel", "parallel", "arbitrary"),
        name="diff_attention",
    )(slopes, lam, gain.reshape(1, BRANCH_WIDTH), u_mm, u_mm, u_mm)


def _recurrent_mixers_kernel(
        g_ref,
        gq_ref, gk_ref, gv_ref, gr_ref, wgu_ref, bg_ref, ggain_ref,
        qk_ref, v_ref, op_ref, cw_ref, cb_ref, gb_ref, gain_ref,
        go_ref, o_ref,
        st_ref, c_ref, n_ref, m_ref, halo_ref, *, tt):
    w = BRANCH_WIDTH
    halo = halo_ref.shape[0]

    @pl.when(pl.program_id(1) == 0)
    def _():
        st_ref[...] = jnp.zeros_like(st_ref)
        c_ref[...] = jnp.zeros_like(c_ref)
        n_ref[...] = jnp.zeros_like(n_ref)
        m_ref[...] = jnp.zeros_like(m_ref)
        halo_ref[...] = jnp.zeros_like(halo_ref)

    dkh = N_HEADS * GLA_DK
    hc = N_HEADS * CHUNK
    iota = lambda shape, dim: lax.broadcasted_iota(jnp.int32, shape, dim)
    tril = (iota((CHUNK, CHUNK), 0) >= iota((CHUNK, CHUNK), 1)).astype(BF16)
    lane = iota((1, LANES), 1)
    head_of_dk_lane = iota((1, dkh), 1) // GLA_DK
    head_of_dv_lane = iota((1, w), 1) // HEAD_DV
    gla_causal = iota((CHUNK, hc), 0) >= iota((CHUNK, hc), 1) % CHUNK
    gla_state_block = iota((w, dkh), 0) // HEAD_DV == iota((w, dkh), 1) // GLA_DK
    r_i, c_i = iota((hc, hc), 0), iota((hc, hc), 1)
    ml_visible = (r_i // CHUNK == c_i // CHUNK) & (r_i % CHUNK >= c_i % CHUNK)

    def per_head_rows(a, width):
        return jnp.concatenate([a[:, h * width:(h + 1) * width] for h in range(N_HEADS)], axis=0)

    def head_blocks(a, head_of_lane):
        return jnp.concatenate([jnp.where(head_of_lane == h, a, jnp.zeros_like(a)) for h in range(N_HEADS)], axis=0)

    def gla_chunk(c):
        r0 = pl.multiple_of(c * CHUNK, CHUNK)
        rows = pl.ds(r0, CHUNK)
        pre = _mm(g_ref[rows, :].astype(BF16), wgu_ref[...]) + bg_ref[...]
        log_a = _log_sigmoid(pre) * (1.0 / GLA_GATE_NORM)
        cum = _tri_cumsum(tril, log_a)
        total = cum[CHUNK - 1:CHUNK, :]
        qf = gq_ref[rows, :].astype(F32) * (GLA_DK ** -0.5)
        kf = gk_ref[rows, :].astype(F32)
        q_dec = (qf * jnp.exp(cum)).astype(BF16)
        k_inv = (kf * jnp.exp(-cum)).astype(BF16)
        k_tail = (kf * jnp.exp(total - cum)).astype(BF16)
        v = gv_ref[rows, :]
        att = jnp.where(gla_causal, _nt(q_dec, head_blocks(k_inv, head_of_dk_lane)), 0.0)
        st = st_ref[...]
        o = _mm(att.astype(BF16), head_blocks(v, head_of_dv_lane)) + _nt(q_dec, st.astype(BF16))
        st_ref[...] = jnp.exp(total) * st + jnp.where(gla_state_block, _tn(v, k_tail), 0.0)
        for h in range(N_HEADS):
            vs = slice(h * HEAD_DV, (h + 1) * HEAD_DV)
            r = gr_ref[rows, vs]
            y = _rms(o[:, vs], ggain_ref[:, vs]) * (r * _sigmoid(r))
            go_ref[rows, vs] = y.astype(go_ref.dtype)

    def mlstm_chunk(c):
        r0 = pl.multiple_of(c * CHUNK, CHUNK)
        rows = pl.ds(r0, CHUNK)
        p0 = pl.multiple_of(jnp.maximum(r0 - halo, 0), halo)
        prev = jnp.where(c > 0, qk_ref[pl.ds(p0, halo), :], halo_ref[...])
        ext = jnp.concatenate([prev, qk_ref[rows, :]], axis=0)
        acc = cb_ref[...]
        for j in range(MLSTM_CONV):
            s0 = halo - (MLSTM_CONV - 1) + j
            acc = acc + cw_ref[j:j + 1, :] * ext[s0:s0 + CHUNK, :]
        qk = acc * _sigmoid(acc)
        gb = g_ref[rows, :] + gb_ref[...]
        log_f = _log_sigmoid(gb)
        bcum = _tri_cumsum(tril, log_f)
        x = jnp.where(lane < GATE_ML_F, gb, bcum)
        xt = x.T
        heads = range(N_HEADS)
        rep = lambda per_head: jnp.concatenate(
            [jnp.broadcast_to(a, (CHUNK, a.shape[-1])) for a in per_head], axis=0)
        bcum_c = jnp.concatenate([bcum[:, GATE_ML_F + h:GATE_ML_F + h + 1] for h in heads], axis=0)
        logi_c = jnp.concatenate([gb[:, GATE_ML_I + h:GATE_ML_I + h + 1] for h in heads], axis=0)
        r_row = jnp.concatenate([xt[GATE_ML_I + h:GATE_ML_I + h + 1, :] - xt[GATE_ML_F + h:GATE_ML_F + h + 1, :]
                                 for h in heads], axis=1)
        btot = [bcum[CHUNK - 1:CHUNK, GATE_ML_F + h:GATE_ML_F + h + 1] for h in heads]
        m_prev = [m_ref[h] for h in heads]
        tail = rep(btot) - bcum_c + logi_c
        m_loc = [jnp.max(tail[h * CHUNK:(h + 1) * CHUNK], axis=0, keepdims=True) for h in heads]
        w_tail = jnp.exp(tail - rep(m_loc))
        q_lanes = qk[:, :w].astype(BF16)
        q = per_head_rows(q_lanes, HEAD_DV)
        kf = per_head_rows(qk[:, w:], HEAD_DV) * (HEAD_DV ** -0.5)
        v = per_head_rows(v_ref[rows, :], HEAD_DV)
        kw = kf * w_tail
        dlog = jnp.where(ml_visible, bcum_c + r_row, NEG)
        inter_log = bcum_c + rep(m_prev)
        m_t = jnp.maximum(jnp.max(dlog, axis=-1, keepdims=True), inter_log)
        a_mat = jnp.exp(dlog - m_t) * _nt(q, kf.astype(BF16))
        inter_w = jnp.exp(inter_log - m_t)
        c_prev = c_ref[...]
        n_prev = [n_ref[h] for h in heads]
        num = _mm(a_mat.astype(BF16), v) + inter_w * _mm(head_blocks(q_lanes, head_of_dv_lane), c_prev.astype(BF16))
        qn = jnp.sum(q.astype(F32) * rep(n_prev), axis=-1, keepdims=True)
        den = jnp.sum(a_mat, axis=-1, keepdims=True) + inter_w * qn
        hh = num / jnp.maximum(jnp.abs(den), jnp.exp(-m_t))
        for h in heads:
            hs = slice(h * HEAD_DV, (h + 1) * HEAD_DV)
            y = _rms(_sigmoid(op_ref[rows, hs]) * hh[h * CHUNK:(h + 1) * CHUNK], gain_ref[:, hs])
            o_ref[rows, hs] = y.astype(o_ref.dtype)
        kw_lanes = jnp.concatenate([kw[h * CHUNK:(h + 1) * CHUNK] for h in heads], axis=1)
        d_c = _tn(head_blocks(kw_lanes.astype(BF16), head_of_dv_lane), v)
        rep_d = lambda per_head: jnp.concatenate(
            [jnp.broadcast_to(a, (HEAD_DV, 1)) for a in per_head], axis=0)
        m_new = [jnp.maximum(btot[h] + m_prev[h], m_loc[h]) for h in heads]
        a = [jnp.exp(btot[h] + m_prev[h] - m_new[h]) for h in heads]
        bb = [jnp.exp(m_loc[h] - m_new[h]) for h in heads]
        c_ref[...] = rep_d(a) * c_prev + rep_d(bb) * d_c
        for h in heads:
            d_n = jnp.sum(kw[h * CHUNK:(h + 1) * CHUNK], axis=0, keepdims=True)
            n_ref[h] = a[h] * n_prev[h] + bb[h] * d_n
            m_ref[h] = m_new[h]

    def chunk(c, _):
        gla_chunk(c)
        mlstm_chunk(c)
        return 0

    lax.fori_loop(0, tt // CHUNK, chunk, 0, unroll=4)
    halo_ref[...] = qk_ref[tt - halo:tt, :]


def recurrent_mixers(u_mm, u_fx, gla_w_gate_up, gla_b_gate, gla_gain, conv_w, conv_b, b_i, b_f, ml_gain,
                     batch, seq, tt=1024):
    w = BRANCH_WIDTH
    dkh = N_HEADS * GLA_DK
    tt = min(tt, seq)
    nt = seq // tt
    wgu = (jnp.zeros((LANES, dkh), F32).at[GATE_GLA_LOW:GATE_GLA_LOW + GLA_GATE_RANK].set(gla_w_gate_up)
           .astype(BF16))
    gate_bias = (jnp.zeros((1, LANES), F32).at[0, GATE_ML_I:GATE_ML_I + N_HEADS].set(b_i)
                 .at[0, GATE_ML_F:GATE_ML_F + N_HEADS].set(b_f))
    const = lambda b, t: (0, 0)
    rows = lambda width, col0: pl.BlockSpec((tt, width), lambda b, t: (b * nt + t, col0 // width))
    out = jax.ShapeDtypeStruct((batch * seq, w), BF16)
    return pl.pallas_call(
        functools.partial(_recurrent_mixers_kernel, tt=tt),
        grid=(batch, nt),
        in_specs=[
            rows(LANES, FX_GATES),
            rows(dkh, MM_GLA_Q), rows(dkh, MM_GLA_K), rows(w, MM_GLA_V), rows(w, FX_GLA_R),
            pl.BlockSpec((LANES, dkh), const), pl.BlockSpec((1, dkh), const), pl.BlockSpec((1, w), const),
            rows(2 * w, FX_ML_QK), rows(w, MM_ML_V), rows(w, FX_ML_O),
            pl.BlockSpec((MLSTM_CONV, 2 * w), const), pl.BlockSpec((1, 2 * w), const),
            pl.BlockSpec((1, LANES), const), pl.BlockSpec((1, w), const),
        ],
        out_specs=[rows(w, 0), rows(w, 0)],
        out_shape=[out, out],
        scratch_shapes=[pltpu.VMEM((w, dkh), F32),
                        pltpu.VMEM((N_HEADS * HEAD_DV, HEAD_DV), F32),
                        pltpu.VMEM((N_HEADS, 1, HEAD_DV), F32),
                        pltpu.VMEM((N_HEADS, 1, 1), F32),
                        pltpu.VMEM((8, 2 * w), F32)],
        compiler_params=_params("parallel", "arbitrary"),
        name="recurrent_mixers",
    )(u_fx,
      u_mm, u_mm, u_mm, u_fx, wgu, gla_b_gate.reshape(1, dkh), gla_gain.reshape(1, w),
      u_fx, u_mm, u_fx, conv_w, conv_b.reshape(1, 2 * w), gate_bias, ml_gain.reshape(1, w))


BAND_Q_CHUNKS = 4
BAND_TQ = BAND_Q_CHUNKS * CHUNK
BAND_TK = (CA_LEFT_CHUNKS + BAND_Q_CHUNKS) * CHUNK
BAND_PAD = CA_LEFT_CHUNKS * CHUNK


def _band_bias_kernel(table_ref, o_ref):
    l = pl.program_id(0)
    h = pl.program_id(1)
    row = lax.broadcasted_iota(jnp.int32, (BAND_TQ, BAND_TK), 0)
    col = lax.broadcasted_iota(jnp.int32, (BAND_TQ, BAND_TK), 1)
    idx = jnp.clip(row - col + BAND_PAD, -REL_CLIP, REL_CLIP) + REL_CLIP
    delta = row // CHUNK + CA_LEFT_CHUNKS - col // CHUNK
    in_band = (delta >= 0) & (delta <= CA_LEFT_CHUNKS)

    table = table_ref[0, 0]
    tiles = []
    for c0 in range(0, BAND_TK, LANES):
        i = idx[:, c0:c0 + LANES]
        val = jnp.zeros(i.shape, F32)
        for ch in range(table.shape[0]):
            src = jnp.broadcast_to(table[ch:ch + 1, :], i.shape)
            looked_up = jnp.take_along_axis(src, i & (LANES - 1), axis=1)
            val = jnp.where(i // LANES == ch, looked_up, val)
        tiles.append(val)
    o_ref[0, 0] = jnp.where(in_band, jnp.concatenate(tiles, axis=1), NEG)


def band_bias(rel_bias):
    depth, heads, n = rel_bias.shape
    chunks = -(-n // LANES)
    table = jnp.pad(rel_bias, ((0, 0), (0, 0), (0, chunks * LANES - n))).reshape(depth, heads, chunks, LANES)
    return pl.pallas_call(
        _band_bias_kernel,
        grid=(depth, N_HEADS),
        in_specs=[pl.BlockSpec((1, 1, chunks, LANES), lambda l, h: (l, h, 0, 0))],
        out_specs=pl.BlockSpec((1, 1, BAND_TQ, BAND_TK), lambda l, h: (l, h, 0, 0)),
        out_shape=jax.ShapeDtypeStruct((depth, N_HEADS, BAND_TQ, BAND_TK), F32),
        compiler_params=_params("parallel", "parallel"),
        name="band_bias",
    )(table)


def _band_attn_kernel(bias_ref, q_ref, k_ref, v_ref, o_ref, kp_ref, vp_ref, *, seq):
    kp_ref[0:BAND_PAD, :] = jnp.zeros((BAND_PAD, HEAD_DV), BF16)
    vp_ref[0:BAND_PAD, 0:HEAD_DV] = jnp.zeros((BAND_PAD, HEAD_DV), BF16)
    kp_ref[BAND_PAD:BAND_PAD + seq, :] = k_ref[...]
    vp_ref[BAND_PAD:BAND_PAD + seq, 0:HEAD_DV] = v_ref[...]
    vp_ref[:, HEAD_DV:] = jnp.ones((BAND_PAD + seq, HEAD_DV), BF16)
    scale = HEAD_DV ** -0.5

    def tile(t, starts_in_pad):
        r0 = t * BAND_TQ
        r0 = r0 if starts_in_pad else pl.multiple_of(r0, BAND_TQ)
        q = q_ref[pl.ds(r0, BAND_TQ), :]
        kb = kp_ref[pl.ds(r0, BAND_TK), :]
        vb = vp_ref[pl.ds(r0, BAND_TK), :]
        s = _nt(q, kb) * scale + bias_ref[0, 0]
        if starts_in_pad:
            col = lax.broadcasted_iota(jnp.int32, (BAND_TQ, BAND_TK), 1)
            s = jnp.where(col + r0 >= BAND_PAD, s, NEG)
        p = jnp.exp(s - jnp.max(s, axis=-1, keepdims=True))
        ov = _mm(p.astype(BF16), vb)
        o_ref[pl.ds(r0, BAND_TQ), :] = (ov[:, :HEAD_DV] / ov[:, HEAD_DV:]).astype(o_ref.dtype)

    n_pad_tiles = BAND_PAD // BAND_TQ
    for t in range(n_pad_tiles):
        tile(t, True)

    def body(t, _):
        tile(t, False)
        return 0

    lax.fori_loop(n_pad_tiles, seq // BAND_TQ, body, 0, unroll=7)


def band_attention(u_mm, bias, layer, batch, seq):
    cq, ck, cv = MM_CA_Q // LANES, MM_CA_K // LANES, MM_CA_V // LANES
    kern = functools.partial(_band_attn_kernel, seq=seq)
    return pl.pallas_call(
        kern,
        grid=(batch, N_HEADS),
        in_specs=[
            pl.BlockSpec((1, 1, BAND_TQ, BAND_TK), lambda b, h: (layer, h, 0, 0)),
            pl.BlockSpec((seq, LANES), lambda b, h: (b, cq + h)),
            pl.BlockSpec((seq, LANES), lambda b, h: (b, ck + h)),
            pl.BlockSpec((seq, LANES), lambda b, h: (b, cv + h)),
        ],
        out_specs=pl.BlockSpec((seq, HEAD_DV), lambda b, h: (b, h)),
        out_shape=jax.ShapeDtypeStruct((batch * seq, BRANCH_WIDTH), BF16),
        scratch_shapes=[pltpu.VMEM((BAND_PAD + seq, HEAD_DV), BF16),
                        pltpu.VMEM((BAND_PAD + seq, 2 * HEAD_DV), BF16)],
        compiler_params=_params("parallel", "parallel"),
        name="band_attention",
    )(bias, u_mm, u_mm, u_mm)


def _gated_merge_kernel(h_ref, ba_ref, bb_ref, bc_ref, bd_ref, wg_ref, bg_ref, wb_ref, o_ref):
    h = h_ref[...]
    merged = None
    for bi, br_ref in enumerate((ba_ref, bb_ref, bc_ref, bd_ref)):
        gate = _sigmoid(_mm(h, wg_ref[bi]) + bg_ref[bi])
        term = gate * _mm(br_ref[...], wb_ref[bi])
        merged = term if merged is None else merged + term
    o_ref[...] = merged.astype(o_ref.dtype)


def gated_merge(h, branches, w_gate, b_gate, w_branch, tm=1024, tn=512):
    m, d = h.shape
    w = BRANCH_WIDTH
    tm = min(tm, m)
    row_d = pl.BlockSpec((tm, d), lambda i, j: (i, 0))
    row_w = pl.BlockSpec((tm, w), lambda i, j: (i, 0))
    return pl.pallas_call(
        _gated_merge_kernel,
        grid=(m // tm, d // tn),
        in_specs=[
            row_d, row_w, row_w, row_w, row_w,
            pl.BlockSpec((4, d, tn), lambda i, j: (0, 0, j)),
            pl.BlockSpec((4, 1, tn), lambda i, j: (0, 0, j)),
            pl.BlockSpec((4, w, tn), lambda i, j: (0, 0, j)),
        ],
        out_specs=pl.BlockSpec((tm, tn), lambda i, j: (i, j)),
        out_shape=jax.ShapeDtypeStruct((m, d), BF16),
        compiler_params=_params("parallel", "arbitrary"),
        name="gated_merge",
    )(h, *branches, w_gate, b_gate.reshape(4, 1, d), w_branch)


def _out_proj_kernel(a_ref, x_ref, wo_ref, gpost_ref, gnext_ref, xo_ref, ho_ref, *, sub):
    for r in range(a_ref.shape[0] // sub):
        rows = slice(r * sub, (r + 1) * sub)
        x_new = x_ref[rows, :] + _rms(_mm(a_ref[rows, :], wo_ref[...]), gpost_ref[...])
        xo_ref[rows, :] = x_new
        ho_ref[rows, :] = _rms(x_new, gnext_ref[...]).astype(ho_ref.dtype)


def out_proj(a, x, w_out, g_post, g_next, tm=512, sub=128):
    m, d = x.shape
    row_d = pl.BlockSpec((tm, d), lambda i: (i, 0))
    vec_d = pl.BlockSpec((1, d), lambda i: (0, 0))
    return pl.pallas_call(
        functools.partial(_out_proj_kernel, sub=min(sub, tm)),
        grid=(m // tm,),
        in_specs=[row_d, row_d, pl.BlockSpec((d, d), lambda i: (0, 0)), vec_d, vec_d],
        out_specs=[row_d, row_d],
        out_shape=[jax.ShapeDtypeStruct((m, d), F32), jax.ShapeDtypeStruct((m, d), BF16)],
        compiler_params=_params("parallel"),
        name="out_proj",
    )(a, x, w_out, g_post.reshape(1, d), g_next.reshape(1, d))


def _ffn_kernel(h_ref, x_hbm, wu_ref, wd_ref, gpost_ref, *rest, emit_next, sub):
    if emit_next:
        gnext_ref, xo_ref, ho_ref, x_ref, x_sem = rest
    else:
        xo_ref, x_ref, x_sem = rest
    i = pl.program_id(0)
    j = pl.program_id(1)
    tm = x_ref.shape[0]

    def x_copy():
        return pltpu.make_async_copy(x_hbm.at[pl.ds(pl.multiple_of(i * tm, tm), tm)], x_ref, x_sem)

    @pl.when(j == 0)
    def _():
        x_copy().start()
        xo_ref[...] = jnp.zeros_like(xo_ref)

    for s in range(wu_ref.shape[1] // sub):
        cols = slice(s * sub, (s + 1) * sub)
        act = jnp.square(jnp.maximum(_mm(h_ref[...], wu_ref[:, cols]), 0.0))
        xo_ref[...] += _mm(act.astype(BF16), wd_ref[cols, :])

    @pl.when(j == pl.num_programs(1) - 1)
    def _():
        x_copy().wait()
        x_new = x_ref[...] + _rms(xo_ref[...], gpost_ref[...])
        xo_ref[...] = x_new
        if emit_next:
            ho_ref[...] = _rms(x_new, gnext_ref[...]).astype(ho_ref.dtype)


def ffn(h, x, w_up, w_down, g_post, g_next=None, tm=1024, tf=1024, sub=512):
    m, d = x.shape
    f = w_up.shape[1]
    tm = min(tm, m)
    emit_next = g_next is not None
    row_d = pl.BlockSpec((tm, d), lambda i, j: (i, 0))
    vec_d = pl.BlockSpec((1, d), lambda i, j: (0, 0))
    in_specs = [row_d, pl.BlockSpec(memory_space=pl.ANY), pl.BlockSpec((d, tf), lambda i, j: (0, j)),
                pl.BlockSpec((tf, d), lambda i, j: (j, 0)), vec_d]
    args = [h, x, w_up, w_down, g_post.reshape(1, d)]
    out_specs = [row_d]
    out_shape = [jax.ShapeDtypeStruct((m, d), F32)]
    if emit_next:
        in_specs.append(vec_d)
        args.append(g_next.reshape(1, d))
        out_specs.append(row_d)
        out_shape.append(jax.ShapeDtypeStruct((m, d), BF16))
    return pl.pallas_call(
        functools.partial(_ffn_kernel, emit_next=emit_next, sub=sub),
        grid=(m // tm, f // tf),
        in_specs=in_specs,
        out_specs=out_specs,
        out_shape=out_shape,
        scratch_shapes=[pltpu.VMEM((tm, d), F32), pltpu.SemaphoreType.DMA(())],
        compiler_params=pltpu.CompilerParams(dimension_semantics=("parallel", "arbitrary"),
                                             vmem_limit_bytes=FFN_VMEM_LIMIT),
        name="ffn",
    )(*args)


def _pack_w_in(w_in_l):
    w = BRANCH_WIDTH
    gla0 = 3 * w
    ml0 = gla0 + 2 * N_HEADS * GLA_DK + w + GLA_GATE_RANK + w
    ca0 = ml0 + 4 * w + 2 * N_HEADS
    c = lambda a, b: w_in_l[:, a:b]
    g_low0 = gla0 + 2 * N_HEADS * GLA_DK + w
    gates = jnp.concatenate([c(g_low0, g_low0 + GLA_GATE_RANK), c(ml0 + 4 * w, ml0 + 4 * w + 2 * N_HEADS)], axis=1)
    gates = jnp.pad(gates, ((0, 0), (0, LANES - gates.shape[1])))
    w_mm = jnp.concatenate([c(0, 3 * w), c(gla0, gla0 + 2 * N_HEADS * GLA_DK + w),
                            c(ml0 + 2 * w, ml0 + 3 * w), c(ca0, ca0 + 3 * w)], axis=1)
    w_fx = jnp.concatenate([c(ml0, ml0 + 2 * w), c(g_low0 + GLA_GATE_RANK, g_low0 + GLA_GATE_RANK + w),
                            c(ml0 + 3 * w, ml0 + 4 * w), gates], axis=1)
    return w_mm, w_fx


def kernel(x, norm_mix_pre, norm_mix_post, w_in, diff_lambda, diff_norm, gla_w_gate_up, gla_b_gate, gla_norm,
           mlstm_conv_w, mlstm_conv_b, mlstm_b_i, mlstm_b_f, mlstm_norm, rel_bias, w_branch, w_gate, b_gate,
           w_out, norm_ffn_pre, norm_ffn_post, w_up, w_down):
    batch, seq, d = x.shape
    depth = w_in.shape[0]
    m = batch * seq
    xf = x.reshape(m, d)
    bias_tiles = band_bias(rel_bias)
    h = rmsnorm_cast(xf, norm_mix_pre[0])
    for l in range(depth):
        w_mm, w_fx = _pack_w_in(w_in[l].astype(BF16))
        u_mm = matmul(h, w_mm, BF16, tm=min(1024, m), tn=MM_COLS // 2, name="in_proj_mm")
        u_fx = matmul(h, w_fx, F32, tm=min(1024, m), tn=FX_COLS, name="in_proj_fx")
        o_gla, o_mlstm = recurrent_mixers(u_mm, u_fx, gla_w_gate_up[l], gla_b_gate[l], gla_norm[l], mlstm_conv_w[l],
                                          mlstm_conv_b[l], mlstm_b_i[l], mlstm_b_f[l], mlstm_norm[l], batch, seq)
        branches = (
            diff_attention(u_mm, diff_lambda[l], diff_norm[l], l, batch, seq),
            o_gla,
            o_mlstm,
            band_attention(u_mm, bias_tiles, l, batch, seq),
        )
        merged = gated_merge(h, branches, w_gate[l].astype(BF16), b_gate[l], w_branch[l].astype(BF16))
        xf, h2 = out_proj(merged, xf, w_out[l].astype(BF16), norm_mix_post[l], norm_ffn_pre[l])
        g_next = norm_mix_pre[l + 1] if l + 1 < depth else None
        outs = ffn(h2, xf, w_up[l].astype(BF16), w_down[l].astype(BF16), norm_ffn_post[l], g_next)
        xf, h = outs if g_next is not None else (outs[0], None)
    return xf.reshape(batch, seq, d)
```

```python
import functools
import math

import jax
import jax.numpy as jnp
from jax import lax
from jax.experimental import pallas as pl
from jax.experimental.pallas import tpu as pltpu

F32 = jnp.float32
BF16 = jnp.bfloat16

CHUNK = 64
BRANCH_WIDTH = 512
N_HEADS = 4
HEAD_DV = 128
DIFF_QK_DIM = 64
GLA_DK = 64
GLA_GATE_RANK = 16
GLA_GATE_NORM = 16.0
MLSTM_CONV = 4
CA_LEFT_CHUNKS = 8
REL_CLIP = 2 * CHUNK
EPS = 1e-6
NEG = -1e30

LANES = 128
VMEM_LIMIT = 56 * 1024 * 1024
FFN_VMEM_LIMIT = 60 * 1024 * 1024

MM_DIFF_Q, MM_DIFF_K, MM_DIFF_V = 0, 512, 1024
MM_GLA_Q, MM_GLA_K, MM_GLA_V = 1536, 1792, 2048
MM_ML_V = 2560
MM_CA_Q, MM_CA_K, MM_CA_V = 3072, 3584, 4096
MM_COLS = 4608
FX_ML_QK, FX_GLA_R, FX_ML_O, FX_GATES = 0, 1024, 1536, 2048
FX_COLS = 2176
GATE_GLA_LOW, GATE_ML_I, GATE_ML_F = 0, 16, 20

SUM_ROWS = 16

NT_DIMS = (((1,), (1,)), ((), ()))
TN_DIMS = (((0,), (0,)), ((), ()))


def _nt(a, b):
    return lax.dot_general(a, b, NT_DIMS, preferred_element_type=F32)


def _tn(a, b):
    return lax.dot_general(a, b, TN_DIMS, preferred_element_type=F32)


def _mm(a, b):
    return jnp.dot(a, b, preferred_element_type=F32)


def _tri_cumsum(tril, x):
    hi = x.astype(BF16)
    rest = x - hi.astype(F32)
    mid = rest.astype(BF16)
    lo = (rest - mid.astype(F32)).astype(BF16)
    n = x.shape[1]
    parts = _mm(tril, jnp.concatenate([hi, mid, lo], axis=1))
    return parts[:, :n] + parts[:, n:2 * n] + parts[:, 2 * n:]


def _rms(x, gain):
    return x * lax.rsqrt(jnp.mean(x * x, axis=-1, keepdims=True) + EPS) * gain


def _log_sigmoid(x):
    return jnp.minimum(x, 0.0) - jnp.log1p(jnp.exp(-jnp.abs(x)))


def _sigmoid(x):
    return 0.5 * jnp.tanh(0.5 * x) + 0.5


def _params(*sem):
    return pltpu.CompilerParams(dimension_semantics=sem, vmem_limit_bytes=VMEM_LIMIT)


def _rmsnorm_cast_kernel(x_ref, g_ref, o_ref):
    o_ref[...] = _rms(x_ref[...], g_ref[...]).astype(o_ref.dtype)


def rmsnorm_cast(x, g, tm=512):
    m, d = x.shape
    return pl.pallas_call(
        _rmsnorm_cast_kernel,
        grid=(m // tm,),
        in_specs=[pl.BlockSpec((tm, d), lambda i: (i, 0)), pl.BlockSpec((1, d), lambda i: (0, 0))],
        out_specs=pl.BlockSpec((tm, d), lambda i: (i, 0)),
        out_shape=jax.ShapeDtypeStruct((m, d), BF16),
        compiler_params=_params("parallel"),
        name="rmsnorm_cast",
    )(x, g.reshape(1, d))


def _matmul_kernel(a_ref, b_ref, o_ref):
    o_ref[...] = _mm(a_ref[...], b_ref[...]).astype(o_ref.dtype)


def matmul(a, b, out_dtype, tm, tn, name):
    m, k = a.shape
    n = b.shape[1]
    return pl.pallas_call(
        _matmul_kernel,
        grid=(m // tm, n // tn),
        in_specs=[pl.BlockSpec((tm, k), lambda i, j: (i, 0)), pl.BlockSpec((k, tn), lambda i, j: (0, j))],
        out_specs=pl.BlockSpec((tm, tn), lambda i, j: (i, j)),
        out_shape=jax.ShapeDtypeStruct((m, n), out_dtype),
        compiler_params=_params("parallel", "arbitrary"),
        name=name,
    )(a, b)


def _diff_attn_kernel(slopes_ref, lam_ref, gain_ref, q_ref, k_ref, v_ref, o_ref,
                      vt_ref, kb_ref, diag_ref, m_ref, acc_ref, *, t, lam_init):
    h = pl.program_id(1)
    qi = pl.program_id(2)
    slope = slopes_ref[h]
    q0 = pl.multiple_of(qi * t, t)
    first_map = lax.broadcasted_iota(jnp.int32, (1, LANES), 1) < DIFF_QK_DIM

    @pl.when(qi == 0)
    def _():
        vt_ref[0:HEAD_DV, :] = v_ref[...].astype(F32).T.astype(BF16)
        vt_ref[HEAD_DV:, :] = jnp.ones((SUM_ROWS, v_ref.shape[0]), BF16)
        key = lax.broadcasted_iota(jnp.int32, (t, t), 0)
        qry = lax.broadcasted_iota(jnp.int32, (t, t), 1)
        past = slope * key.astype(F32)
        visible = key < (qry // CHUNK + 1) * CHUNK
        diag = jnp.where(visible, slope * (qry - jnp.abs(qry - key)).astype(F32), NEG)
        for mi in range(2):
            kb_ref[:, mi * t:(mi + 1) * t] = past
            diag_ref[:, mi * t:(mi + 1) * t] = diag

    q = q_ref[...] * jnp.asarray(DIFF_QK_DIM ** -0.5, BF16)
    q_maps = jnp.concatenate([jnp.where(first_map, q, jnp.zeros_like(q)),
                              jnp.where(first_map, jnp.zeros_like(q), q)], axis=0)
    m_ref[...] = jnp.full(m_ref.shape, NEG, F32)
    acc_ref[...] = jnp.zeros_like(acc_ref)

    def accumulate(z, shift, vt):
        m_prev = m_ref[...]
        m_new = jnp.maximum(m_prev, jnp.max(z, axis=0, keepdims=True) + shift)
        p = jnp.exp(z - (m_new - shift))
        acc_ref[...] = jnp.exp(m_prev - m_new) * acc_ref[...] + _mm(vt, p.astype(BF16))
        m_ref[...] = m_new

    def past_tile(j, _):
        k0 = pl.multiple_of(j * t, t)
        z = _nt(k_ref[pl.ds(k0, t), :], q_maps) + kb_ref[...]
        accumulate(z, slope * (k0 - q0).astype(F32), vt_ref[:, pl.ds(k0, t)])
        return 0

    def two_past_tiles(jj, _):
        past_tile(2 * jj, 0)
        past_tile(2 * jj + 1, 0)
        return 0

    lax.fori_loop(0, qi // 2, two_past_tiles, 0)

    @pl.when(qi % 2 == 1)
    def _():
        past_tile(qi - 1, 0)

    z = _nt(k_ref[pl.ds(q0, t), :], q_maps) + diag_ref[...]
    accumulate(z, 0.0, vt_ref[:, pl.ds(q0, t)])

    lam = lam_ref[...]
    lam_full = (jnp.exp(jnp.sum(lam[0:1] * lam[1:2], axis=-1, keepdims=True))
                - jnp.exp(jnp.sum(lam[2:3] * lam[3:4], axis=-1, keepdims=True)) + lam_init)
    acc = acc_ref[...]
    ratio = acc[:HEAD_DV] / acc[HEAD_DV:HEAD_DV + 1]
    o = (ratio[:, :t] - lam_full * ratio[:, t:]).T
    o_ref[...] = (_rms(o, gain_ref[...]) * (1.0 - lam_init)).astype(o_ref.dtype)


def diff_attention(u_mm, lam, gain, layer_idx, batch, seq, tq=512):
    lam_init = 0.8 - 0.6 * math.exp(-0.3 * layer_idx)
    nq = seq // tq
    slopes = jnp.asarray([2.0 ** (-8.0 * (i + 1) / N_HEADS) for i in range(N_HEADS)], F32)
    cq, ck, cv = MM_DIFF_Q // LANES, MM_DIFF_K // LANES, MM_DIFF_V // LANES
    kern = functools.partial(_diff_attn_kernel, t=tq, lam_init=lam_init)
    return pl.pallas_call(
        kern,
        grid=(batch, N_HEADS, nq),
        in_specs=[
            pl.BlockSpec(memory_space=pltpu.SMEM),
            pl.BlockSpec((4, DIFF_QK_DIM), lambda b, h, i: (0, 0)),
            pl.BlockSpec((1, HEAD_DV), lambda b, h, i: (0, h)),
            pl.BlockSpec((tq, LANES), lambda b, h, i: (b * nq + i, cq + h)),
            pl.BlockSpec((seq, LANES), lambda b, h, i: (b, ck + h)),
            pl.BlockSpec((seq, LANES), lambda b, h, i: (b, cv + h)),
        ],
        out_specs=pl.BlockSpec((tq, HEAD_DV), lambda b, h, i: (b * nq + i, h)),
        out_shape=jax.ShapeDtypeStruct((batch * seq, BRANCH_WIDTH), BF16),
        scratch_shapes=[pltpu.VMEM((HEAD_DV + SUM_ROWS, seq), BF16),
                        pltpu.VMEM((tq, 2 * tq), F32),
                        pltpu.VMEM((tq, 2 * tq), F32),
                        pltpu.VMEM((1, 2 * tq), F32),
                        pltpu.VMEM((HEAD_DV + SUM_ROWS, 2 * tq), F32)],
        compiler_params=_params("parallel", "parallel", "arbitrary"),
        name="diff_attention",
    )(slopes, lam, gain.reshape(1, BRANCH_WIDTH), u_mm, u_mm, u_mm)


def _recurrent_mixers_kernel(
        g_ref,
        gq_ref, gk_ref, gv_ref, gr_ref, wgu_ref, bg_ref, ggain_ref,
        qk_ref, v_ref, op_ref, cw_ref, cb_ref, gb_ref, gain_ref,
        go_ref, o_ref,
        st_ref, c_ref, n_ref, m_ref, halo_ref, *, tt):
    w = BRANCH_WIDTH
    halo = halo_ref.shape[0]

    @pl.when(pl.program_id(1) == 0)
    def _():
        st_ref[...] = jnp.zeros_like(st_ref)
        c_ref[...] = jnp.zeros_like(c_ref)
        n_ref[...] = jnp.zeros_like(n_ref)
        m_ref[...] = jnp.zeros_like(m_ref)
        halo_ref[...] = jnp.zeros_like(halo_ref)

    dkh = N_HEADS * GLA_DK
    hc = N_HEADS * CHUNK
    iota = lambda shape, dim: lax.broadcasted_iota(jnp.int32, shape, dim)
    tril = (iota((CHUNK, CHUNK), 0) >= iota((CHUNK, CHUNK), 1)).astype(BF16)
    lane = iota((1, LANES), 1)
    head_of_dk_lane = iota((1, dkh), 1) // GLA_DK
    head_of_dv_lane = iota((1, w), 1) // HEAD_DV
    gla_causal = iota((CHUNK, hc), 0) >= iota((CHUNK, hc), 1) % CHUNK
    gla_state_block = iota((w, dkh), 0) // HEAD_DV == iota((w, dkh), 1) // GLA_DK
    r_i, c_i = iota((hc, hc), 0), iota((hc, hc), 1)
    ml_visible = (r_i // CHUNK == c_i // CHUNK) & (r_i % CHUNK >= c_i % CHUNK)

    def per_head_rows(a, width):
        return jnp.concatenate([a[:, h * width:(h + 1) * width] for h in range(N_HEADS)], axis=0)

    def head_blocks(a, head_of_lane):
        return jnp.concatenate([jnp.where(head_of_lane == h, a, jnp.zeros_like(a)) for h in range(N_HEADS)], axis=0)

    def gla_chunk(c):
        r0 = pl.multiple_of(c * CHUNK, CHUNK)
        rows = pl.ds(r0, CHUNK)
        pre = _mm(g_ref[rows, :].astype(BF16), wgu_ref[...]) + bg_ref[...]
        log_a = _log_sigmoid(pre) * (1.0 / GLA_GATE_NORM)
        cum = _tri_cumsum(tril, log_a)
        total = cum[CHUNK - 1:CHUNK, :]
        qf = gq_ref[rows, :].astype(F32) * (GLA_DK ** -0.5)
        kf = gk_ref[rows, :].astype(F32)
        q_dec = (qf * jnp.exp(cum)).astype(BF16)
        k_inv = (kf * jnp.exp(-cum)).astype(BF16)
        k_tail = (kf * jnp.exp(total - cum)).astype(BF16)
        v = gv_ref[rows, :]
        att = jnp.where(gla_causal, _nt(q_dec, head_blocks(k_inv, head_of_dk_lane)), 0.0)
        st = st_ref[...]
        o = _mm(att.astype(BF16), head_blocks(v, head_of_dv_lane)) + _nt(q_dec, st.astype(BF16))
        st_ref[...] = jnp.exp(total) * st + jnp.where(gla_state_block, _tn(v, k_tail), 0.0)
        for h in range(N_HEADS):
            vs = slice(h * HEAD_DV, (h + 1) * HEAD_DV)
            r = gr_ref[rows, vs]
            y = _rms(o[:, vs], ggain_ref[:, vs]) * (r * _sigmoid(r))
            go_ref[rows, vs] = y.astype(go_ref.dtype)

    def mlstm_chunk(c):
        r0 = pl.multiple_of(c * CHUNK, CHUNK)
        rows = pl.ds(r0, CHUNK)
        p0 = pl.multiple_of(jnp.maximum(r0 - halo, 0), halo)
        prev = jnp.where(c > 0, qk_ref[pl.ds(p0, halo), :], halo_ref[...])
        ext = jnp.concatenate([prev, qk_ref[rows, :]], axis=0)
        acc = cb_ref[...]
        for j in range(MLSTM_CONV):
            s0 = halo - (MLSTM_CONV - 1) + j
            acc = acc + cw_ref[j:j + 1, :] * ext[s0:s0 + CHUNK, :]
        qk = acc * _sigmoid(acc)
        gb = g_ref[rows, :] + gb_ref[...]
        log_f = _log_sigmoid(gb)
        bcum = _tri_cumsum(tril, log_f)
        x = jnp.where(lane < GATE_ML_F, gb, bcum)
        xt = x.T
        heads = range(N_HEADS)
        rep = lambda per_head: jnp.concatenate(
            [jnp.broadcast_to(a, (CHUNK, a.shape[-1])) for a in per_head], axis=0)
        bcum_c = jnp.concatenate([bcum[:, GATE_ML_F + h:GATE_ML_F + h + 1] for h in heads], axis=0)
        logi_c = jnp.concatenate([gb[:, GATE_ML_I + h:GATE_ML_I + h + 1] for h in heads], axis=0)
        r_row = jnp.concatenate([xt[GATE_ML_I + h:GATE_ML_I + h + 1, :] - xt[GATE_ML_F + h:GATE_ML_F + h + 1, :]
                                 for h in heads], axis=1)
        btot = [bcum[CHUNK - 1:CHUNK, GATE_ML_F + h:GATE_ML_F + h + 1] for h in heads]
        m_prev = [m_ref[h] for h in heads]
        tail = rep(btot) - bcum_c + logi_c
        m_loc = [jnp.max(tail[h * CHUNK:(h + 1) * CHUNK], axis=0, keepdims=True) for h in heads]
        w_tail = jnp.exp(tail - rep(m_loc))
        q_lanes = qk[:, :w].astype(BF16)
        q = per_head_rows(q_lanes, HEAD_DV)
        kf = per_head_rows(qk[:, w:], HEAD_DV) * (HEAD_DV ** -0.5)
        v = per_head_rows(v_ref[rows, :], HEAD_DV)
        kw = kf * w_tail
        dlog = jnp.where(ml_visible, bcum_c + r_row, NEG)
        inter_log = bcum_c + rep(m_prev)
        m_t = jnp.maximum(jnp.max(dlog, axis=-1, keepdims=True), inter_log)
        a_mat = jnp.exp(dlog - m_t) * _nt(q, kf.astype(BF16))
        inter_w = jnp.exp(inter_log - m_t)
        c_prev = c_ref[...]
        n_prev = [n_ref[h] for h in heads]
        num = _mm(a_mat.astype(BF16), v) + inter_w * _mm(head_blocks(q_lanes, head_of_dv_lane), c_prev.astype(BF16))
        qn = jnp.sum(q.astype(F32) * rep(n_prev), axis=-1, keepdims=True)
        den = jnp.sum(a_mat, axis=-1, keepdims=True) + inter_w * qn
        hh = num / jnp.maximum(jnp.abs(den), jnp.exp(-m_t))
        for h in heads:
            hs = slice(h * HEAD_DV, (h + 1) * HEAD_DV)
            y = _rms(_sigmoid(op_ref[rows, hs]) * hh[h * CHUNK:(h + 1) * CHUNK], gain_ref[:, hs])
            o_ref[rows, hs] = y.astype(o_ref.dtype)
        kw_lanes = jnp.concatenate([kw[h * CHUNK:(h + 1) * CHUNK] for h in heads], axis=1)
        d_c = _tn(head_blocks(kw_lanes.astype(BF16), head_of_dv_lane), v)
        rep_d = lambda per_head: jnp.concatenate(
            [jnp.broadcast_to(a, (HEAD_DV, 1)) for a in per_head], axis=0)
        m_new = [jnp.maximum(btot[h] + m_prev[h], m_loc[h]) for h in heads]
        a = [jnp.exp(btot[h] + m_prev[h] - m_new[h]) for h in heads]
        bb = [jnp.exp(m_loc[h] - m_new[h]) for h in heads]
        c_ref[...] = rep_d(a) * c_prev + rep_d(bb) * d_c
        for h in heads:
            d_n = jnp.sum(kw[h * CHUNK:(h + 1) * CHUNK], axis=0, keepdims=True)
            n_ref[h] = a[h] * n_prev[h] + bb[h] * d_n
            m_ref[h] = m_new[h]

    def chunk(c, _):
        gla_chunk(c)
        mlstm_chunk(c)
        return 0

    lax.fori_loop(0, tt // CHUNK, chunk, 0, unroll=4)
    halo_ref[...] = qk_ref[tt - halo:tt, :]


def recurrent_mixers(u_mm, u_fx, gla_w_gate_up, gla_b_gate, gla_gain, conv_w, conv_b, b_i, b_f, ml_gain,
                     batch, seq, tt=1024):
    w = BRANCH_WIDTH
    dkh = N_HEADS * GLA_DK
    tt = min(tt, seq)
    nt = seq // tt
    wgu = (jnp.zeros((LANES, dkh), F32).at[GATE_GLA_LOW:GATE_GLA_LOW + GLA_GATE_RANK].set(gla_w_gate_up)
           .astype(BF16))
    gate_bias = (jnp.zeros((1, LANES), F32).at[0, GATE_ML_I:GATE_ML_I + N_HEADS].set(b_i)
                 .at[0, GATE_ML_F:GATE_ML_F + N_HEADS].set(b_f))
    const = lambda b, t: (0, 0)
    rows = lambda width, col0: pl.BlockSpec((tt, width), lambda b, t: (b * nt + t, col0 // width))
    out = jax.ShapeDtypeStruct((batch * seq, w), BF16)
    return pl.pallas_call(
        functools.partial(_recurrent_mixers_kernel, tt=tt),
        grid=(batch, nt),
        in_specs=[
            rows(LANES, FX_GATES),
            rows(dkh, MM_GLA_Q), rows(dkh, MM_GLA_K), rows(w, MM_GLA_V), rows(w, FX_GLA_R),
            pl.BlockSpec((LANES, dkh), const), pl.BlockSpec((1, dkh), const), pl.BlockSpec((1, w), const),
            rows(2 * w, FX_ML_QK), rows(w, MM_ML_V), rows(w, FX_ML_O),
            pl.BlockSpec((MLSTM_CONV, 2 * w), const), pl.BlockSpec((1, 2 * w), const),
            pl.BlockSpec((1, LANES), const), pl.BlockSpec((1, w), const),
        ],
        out_specs=[rows(w, 0), rows(w, 0)],
        out_shape=[out, out],
        scratch_shapes=[pltpu.VMEM((w, dkh), F32),
                        pltpu.VMEM((N_HEADS * HEAD_DV, HEAD_DV), F32),
                        pltpu.VMEM((N_HEADS, 1, HEAD_DV), F32),
                        pltpu.VMEM((N_HEADS, 1, 1), F32),
                        pltpu.VMEM((8, 2 * w), F32)],
        compiler_params=_params("parallel", "arbitrary"),
        name="recurrent_mixers",
    )(u_fx,
      u_mm, u_mm, u_mm, u_fx, wgu, gla_b_gate.reshape(1, dkh), gla_gain.reshape(1, w),
      u_fx, u_mm, u_fx, conv_w, conv_b.reshape(1, 2 * w), gate_bias, ml_gain.reshape(1, w))


BAND_Q_CHUNKS = 4
BAND_TQ = BAND_Q_CHUNKS * CHUNK
BAND_TK = (CA_LEFT_CHUNKS + BAND_Q_CHUNKS) * CHUNK
BAND_PAD = CA_LEFT_CHUNKS * CHUNK


def _band_bias_kernel(table_ref, o_ref):
    l = pl.program_id(0)
    h = pl.program_id(1)
    row = lax.broadcasted_iota(jnp.int32, (BAND_TQ, BAND_TK), 0)
    col = lax.broadcasted_iota(jnp.int32, (BAND_TQ, BAND_TK), 1)
    idx = jnp.clip(row - col + BAND_PAD, -REL_CLIP, REL_CLIP) + REL_CLIP
    delta = row // CHUNK + CA_LEFT_CHUNKS - col // CHUNK
    in_band = (delta >= 0) & (delta <= CA_LEFT_CHUNKS)

    table = table_ref[0, 0]
    tiles = []
    for c0 in range(0, BAND_TK, LANES):
        i = idx[:, c0:c0 + LANES]
        val = jnp.zeros(i.shape, F32)
        for ch in range(table.shape[0]):
            src = jnp.broadcast_to(table[ch:ch + 1, :], i.shape)
            looked_up = jnp.take_along_axis(src, i & (LANES - 1), axis=1)
            val = jnp.where(i // LANES == ch, looked_up, val)
        tiles.append(val)
    o_ref[0, 0] = jnp.where(in_band, jnp.concatenate(tiles, axis=1), NEG)


def band_bias(rel_bias):
    depth, heads, n = rel_bias.shape
    chunks = -(-n // LANES)
    table = jnp.pad(rel_bias, ((0, 0), (0, 0), (0, chunks * LANES - n))).reshape(depth, heads, chunks, LANES)
    return pl.pallas_call(
        _band_bias_kernel,
        grid=(depth, N_HEADS),
        in_specs=[pl.BlockSpec((1, 1, chunks, LANES), lambda l, h: (l, h, 0, 0))],
        out_specs=pl.BlockSpec((1, 1, BAND_TQ, BAND_TK), lambda l, h: (l, h, 0, 0)),
        out_shape=jax.ShapeDtypeStruct((depth, N_HEADS, BAND_TQ, BAND_TK), F32),
        compiler_params=_params("parallel", "parallel"),
        name="band_bias",
    )(table)


def _band_attn_kernel(bias_ref, q_ref, k_ref, v_ref, o_ref, kp_ref, vp_ref, *, seq):
    kp_ref[0:BAND_PAD, :] = jnp.zeros((BAND_PAD, HEAD_DV), BF16)
    vp_ref[0:BAND_PAD, 0:HEAD_DV] = jnp.zeros((BAND_PAD, HEAD_DV), BF16)
    kp_ref[BAND_PAD:BAND_PAD + seq, :] = k_ref[...]
    vp_ref[BAND_PAD:BAND_PAD + seq, 0:HEAD_DV] = v_ref[...]
    vp_ref[:, HEAD_DV:] = jnp.ones((BAND_PAD + seq, HEAD_DV), BF16)
    scale = HEAD_DV ** -0.5

    def tile(t, starts_in_pad):
        r0 = t * BAND_TQ
        r0 = r0 if starts_in_pad else pl.multiple_of(r0, BAND_TQ)
        q = q_ref[pl.ds(r0, BAND_TQ), :]
        kb = kp_ref[pl.ds(r0, BAND_TK), :]
        vb = vp_ref[pl.ds(r0, BAND_TK), :]
        s = _nt(q, kb) * scale + bias_ref[0, 0]
        if starts_in_pad:
            col = lax.broadcasted_iota(jnp.int32, (BAND_TQ, BAND_TK), 1)
            s = jnp.where(col + r0 >= BAND_PAD, s, NEG)
        p = jnp.exp(s - jnp.max(s, axis=-1, keepdims=True))
        ov = _mm(p.astype(BF16), vb)
        o_ref[pl.ds(r0, BAND_TQ), :] = (ov[:, :HEAD_DV] / ov[:, HEAD_DV:]).astype(o_ref.dtype)

    n_pad_tiles = BAND_PAD // BAND_TQ
    for t in range(n_pad_tiles):
        tile(t, True)

    def body(t, _):
        tile(t, False)
        return 0

    lax.fori_loop(n_pad_tiles, seq // BAND_TQ, body, 0, unroll=7)


def band_attention(u_mm, bias, layer, batch, seq):
    cq, ck, cv = MM_CA_Q // LANES, MM_CA_K // LANES, MM_CA_V // LANES
    kern = functools.partial(_band_attn_kernel, seq=seq)
    return pl.pallas_call(
        kern,
        grid=(batch, N_HEADS),
        in_specs=[
            pl.BlockSpec((1, 1, BAND_TQ, BAND_TK), lambda b, h: (layer, h, 0, 0)),
            pl.BlockSpec((seq, LANES), lambda b, h: (b, cq + h)),
            pl.BlockSpec((seq, LANES), lambda b, h: (b, ck + h)),
            pl.BlockSpec((seq, LANES), lambda b, h: (b, cv + h)),
        ],
        out_specs=pl.BlockSpec((seq, HEAD_DV), lambda b, h: (b, h)),
        out_shape=jax.ShapeDtypeStruct((batch * seq, BRANCH_WIDTH), BF16),
        scratch_shapes=[pltpu.VMEM((BAND_PAD + seq, HEAD_DV), BF16),
                        pltpu.VMEM((BAND_PAD + seq, 2 * HEAD_DV), BF16)],
        compiler_params=_params("parallel", "parallel"),
        name="band_attention",
    )(bias, u_mm, u_mm, u_mm)


def _gated_merge_kernel(h_ref, ba_ref, bb_ref, bc_ref, bd_ref, wg_ref, bg_ref, wb_ref, o_ref):
    h = h_ref[...]
    merged = None
    for bi, br_ref in enumerate((ba_ref, bb_ref, bc_ref, bd_ref)):
        gate = _sigmoid(_mm(h, wg_ref[bi]) + bg_ref[bi])
        term = gate * _mm(br_ref[...], wb_ref[bi])
        merged = term if merged is None else merged + term
    o_ref[...] = merged.astype(o_ref.dtype)


def gated_merge(h, branches, w_gate, b_gate, w_branch, layer, tm=1024, tn=512):
    m, d = h.shape
    w = BRANCH_WIDTH
    tm = min(tm, m)
    row_d = pl.BlockSpec((tm, d), lambda i, j: (i, 0))
    row_w = pl.BlockSpec((tm, w), lambda i, j: (i, 0))
    return pl.pallas_call(
        _gated_merge_kernel,
        grid=(m // tm, d // tn),
        in_specs=[
            row_d, row_w, row_w, row_w, row_w,
            pl.BlockSpec((None, 4, d, tn), lambda i, j: (layer, 0, 0, j)),
            pl.BlockSpec((None, 4, 1, tn), lambda i, j: (layer, 0, 0, j)),
            pl.BlockSpec((None, 4, w, tn), lambda i, j: (layer, 0, 0, j)),
        ],
        out_specs=pl.BlockSpec((tm, tn), lambda i, j: (i, j)),
        out_shape=jax.ShapeDtypeStruct((m, d), BF16),
        compiler_params=_params("parallel", "arbitrary"),
        name="gated_merge",
    )(h, *branches, w_gate, b_gate.reshape(-1, 4, 1, d), w_branch)


def _out_proj_kernel(a_ref, x_ref, wo_ref, gpost_ref, gnext_ref, xo_ref, ho_ref, *, sub):
    for r in range(a_ref.shape[0] // sub):
        rows = slice(r * sub, (r + 1) * sub)
        x_new = x_ref[rows, :] + _rms(_mm(a_ref[rows, :], wo_ref[...]), gpost_ref[...])
        xo_ref[rows, :] = x_new
        ho_ref[rows, :] = _rms(x_new, gnext_ref[...]).astype(ho_ref.dtype)


def out_proj(a, x, w_out, layer, g_post, g_next, tm=512, sub=128):
    m, d = x.shape
    row_d = pl.BlockSpec((tm, d), lambda i: (i, 0))
    vec_d = pl.BlockSpec((1, d), lambda i: (0, 0))
    return pl.pallas_call(
        functools.partial(_out_proj_kernel, sub=min(sub, tm)),
        grid=(m // tm,),
        in_specs=[row_d, row_d, pl.BlockSpec((None, d, d), lambda i: (layer, 0, 0)), vec_d, vec_d],
        out_specs=[row_d, row_d],
        out_shape=[jax.ShapeDtypeStruct((m, d), F32), jax.ShapeDtypeStruct((m, d), BF16)],
        compiler_params=_params("parallel"),
        name="out_proj",
    )(a, x, w_out, g_post.reshape(1, d), g_next.reshape(1, d))


def _ffn_kernel(h_ref, x_hbm, wu_ref, wd_ref, gpost_ref, *rest, emit_next, sub):
    if emit_next:
        gnext_ref, xo_ref, ho_ref, x_ref, x_sem = rest
    else:
        xo_ref, x_ref, x_sem = rest
    i = pl.program_id(0)
    j = pl.program_id(1)
    tm = x_ref.shape[0]

    def x_copy():
        return pltpu.make_async_copy(x_hbm.at[pl.ds(pl.multiple_of(i * tm, tm), tm)], x_ref, x_sem)

    @pl.when(j == 0)
    def _():
        x_copy().start()
        xo_ref[...] = jnp.zeros_like(xo_ref)

    for s in range(wu_ref.shape[1] // sub):
        cols = slice(s * sub, (s + 1) * sub)
        act = jnp.square(jnp.maximum(_mm(h_ref[...], wu_ref[:, cols]), 0.0))
        xo_ref[...] += _mm(act.astype(BF16), wd_ref[cols, :])

    @pl.when(j == pl.num_programs(1) - 1)
    def _():
        x_copy().wait()
        x_new = x_ref[...] + _rms(xo_ref[...], gpost_ref[...])
        xo_ref[...] = x_new
        if emit_next:
            ho_ref[...] = _rms(x_new, gnext_ref[...]).astype(ho_ref.dtype)


def ffn(h, x, w_up, w_down, layer, g_post, g_next=None, tm=1024, tf=1024, sub=512):
    m, d = x.shape
    f = w_up.shape[2]
    tm = min(tm, m)
    emit_next = g_next is not None
    row_d = pl.BlockSpec((tm, d), lambda i, j: (i, 0))
    vec_d = pl.BlockSpec((1, d), lambda i, j: (0, 0))
    in_specs = [row_d, pl.BlockSpec(memory_space=pl.ANY), pl.BlockSpec((None, d, tf), lambda i, j: (layer, 0, j)),
                pl.BlockSpec((None, tf, d), lambda i, j: (layer, j, 0)), vec_d]
    args = [h, x, w_up, w_down, g_post.reshape(1, d)]
    out_specs = [row_d]
    out_shape = [jax.ShapeDtypeStruct((m, d), F32)]
    if emit_next:
        in_specs.append(vec_d)
        args.append(g_next.reshape(1, d))
        out_specs.append(row_d)
        out_shape.append(jax.ShapeDtypeStruct((m, d), BF16))
    return pl.pallas_call(
        functools.partial(_ffn_kernel, emit_next=emit_next, sub=sub),
        grid=(m // tm, f // tf),
        in_specs=in_specs,
        out_specs=out_specs,
        out_shape=out_shape,
        scratch_shapes=[pltpu.VMEM((tm, d), F32), pltpu.SemaphoreType.DMA(())],
        compiler_params=pltpu.CompilerParams(dimension_semantics=("parallel", "arbitrary"),
                                             vmem_limit_bytes=FFN_VMEM_LIMIT),
        name="ffn",
    )(*args)


def _pack_w_in(w_in_l):
    w = BRANCH_WIDTH
    gla0 = 3 * w
    ml0 = gla0 + 2 * N_HEADS * GLA_DK + w + GLA_GATE_RANK + w
    ca0 = ml0 + 4 * w + 2 * N_HEADS
    c = lambda a, b: w_in_l[:, a:b]
    g_low0 = gla0 + 2 * N_HEADS * GLA_DK + w
    gates = jnp.concatenate([c(g_low0, g_low0 + GLA_GATE_RANK), c(ml0 + 4 * w, ml0 + 4 * w + 2 * N_HEADS)], axis=1)
    gates = jnp.pad(gates, ((0, 0), (0, LANES - gates.shape[1])))
    w_mm = jnp.concatenate([c(0, 3 * w), c(gla0, gla0 + 2 * N_HEADS * GLA_DK + w),
                            c(ml0 + 2 * w, ml0 + 3 * w), c(ca0, ca0 + 3 * w)], axis=1)
    w_fx = jnp.concatenate([c(ml0, ml0 + 2 * w), c(g_low0 + GLA_GATE_RANK, g_low0 + GLA_GATE_RANK + w),
                            c(ml0 + 3 * w, ml0 + 4 * w), gates], axis=1)
    return w_mm, w_fx


def kernel(x, norm_mix_pre, norm_mix_post, w_in, diff_lambda, diff_norm, gla_w_gate_up, gla_b_gate, gla_norm,
           mlstm_conv_w, mlstm_conv_b, mlstm_b_i, mlstm_b_f, mlstm_norm, rel_bias, w_branch, w_gate, b_gate,
           w_out, norm_ffn_pre, norm_ffn_post, w_up, w_down):
    batch, seq, d = x.shape
    depth = w_in.shape[0]
    m = batch * seq
    xf = x.reshape(m, d)
    bias_tiles = band_bias(rel_bias)
    w_gate, w_branch, w_out, w_up, w_down = (a.astype(BF16) for a in (w_gate, w_branch, w_out, w_up, w_down))
    h = rmsnorm_cast(xf, norm_mix_pre[0])
    for l in range(depth):
        w_mm, w_fx = _pack_w_in(w_in[l].astype(BF16))
        u_mm = matmul(h, w_mm, BF16, tm=min(1024, m), tn=MM_COLS // 2, name="in_proj_mm")
        u_fx = matmul(h, w_fx, F32, tm=min(1024, m), tn=FX_COLS, name="in_proj_fx")
        o_gla, o_mlstm = recurrent_mixers(u_mm, u_fx, gla_w_gate_up[l], gla_b_gate[l], gla_norm[l], mlstm_conv_w[l],
                                          mlstm_conv_b[l], mlstm_b_i[l], mlstm_b_f[l], mlstm_norm[l], batch, seq)
        branches = (
            diff_attention(u_mm, diff_lambda[l], diff_norm[l], l, batch, seq),
            o_gla,
            o_mlstm,
            band_attention(u_mm, bias_tiles, l, batch, seq),
        )
        merged = gated_merge(h, branches, w_gate, b_gate, w_branch, l)
        xf, h2 = out_proj(merged, xf, w_out, l, norm_mix_post[l], norm_ffn_pre[l])
        g_next = norm_mix_pre[l + 1] if l + 1 < depth else None
        outs = ffn(h2, xf, w_up, w_down, l, norm_ffn_post[l], g_next)
        xf, h = outs if g_next is not None else (outs[0], None)
    return xf.reshape(batch, seq, d)
```

```python
import functools
import math

import jax
import jax.numpy as jnp
from jax import lax
from jax.experimental import pallas as pl
from jax.experimental.pallas import tpu as pltpu

F32 = jnp.float32
BF16 = jnp.bfloat16

CHUNK = 64
BRANCH_WIDTH = 512
N_HEADS = 4
HEAD_DV = 128
DIFF_QK_DIM = 64
GLA_DK = 64
GLA_GATE_RANK = 16
GLA_GATE_NORM = 16.0
MLSTM_CONV = 4
CA_LEFT_CHUNKS = 8
REL_CLIP = 2 * CHUNK
EPS = 1e-6
NEG = -1e30

LANES = 128
SUBLANES = 8
VMEM_LIMIT = 56 * 1024 * 1024
FFN_VMEM_LIMIT = 60 * 1024 * 1024

MM_DIFF_Q, MM_DIFF_K, MM_DIFF_V = 0, 512, 1024
MM_GLA_Q, MM_GLA_K, MM_GLA_V = 1536, 1792, 2048
MM_ML_V = 2560
MM_CA_Q, MM_CA_K, MM_CA_V = 3072, 3584, 4096
MM_COLS = 4608
FX_ML_QK, FX_GLA_R, FX_ML_O, FX_GATES = 0, 1024, 1536, 2048
FX_COLS = 2176
GATE_GLA_LOW, GATE_ML_I, GATE_ML_F = 0, 16, 20

SUM_ROWS = 16

NT_DIMS = (((1,), (1,)), ((), ()))
TN_DIMS = (((0,), (0,)), ((), ()))


def _nt(a, b):
    return lax.dot_general(a, b, NT_DIMS, preferred_element_type=F32)


def _tn(a, b):
    return lax.dot_general(a, b, TN_DIMS, preferred_element_type=F32)


def _mm(a, b):
    return jnp.dot(a, b, preferred_element_type=F32)


def _tri_cumsum(tril, x):
    hi = x.astype(BF16)
    rest = x - hi.astype(F32)
    mid = rest.astype(BF16)
    lo = (rest - mid.astype(F32)).astype(BF16)
    n = x.shape[1]
    parts = _mm(tril, jnp.concatenate([hi, mid, lo], axis=1))
    return parts[:, :n] + parts[:, n:2 * n] + parts[:, 2 * n:]


def _rms(x, gain):
    return x * lax.rsqrt(jnp.mean(x * x, axis=-1, keepdims=True) + EPS) * gain


def _log_sigmoid(x):
    return jnp.minimum(x, 0.0) - jnp.log1p(jnp.exp(-jnp.abs(x)))


def _sigmoid(x):
    return 0.5 * jnp.tanh(0.5 * x) + 0.5


def _params(*sem):
    return pltpu.CompilerParams(dimension_semantics=sem, vmem_limit_bytes=VMEM_LIMIT)


def _rmsnorm_cast_kernel(x_ref, g_ref, o_ref):
    o_ref[...] = _rms(x_ref[...], g_ref[...]).astype(o_ref.dtype)


def rmsnorm_cast(x, g, tm=1024):
    m, d = x.shape
    tm = min(tm, m)
    return pl.pallas_call(
        _rmsnorm_cast_kernel,
        grid=(m // tm,),
        in_specs=[pl.BlockSpec((tm, d), lambda i: (i, 0)), pl.BlockSpec((1, d), lambda i: (0, 0))],
        out_specs=pl.BlockSpec((tm, d), lambda i: (i, 0)),
        out_shape=jax.ShapeDtypeStruct((m, d), BF16),
        compiler_params=_params("parallel"),
        name="rmsnorm_cast",
    )(x, g.reshape(1, d))


def _matmul_kernel(a_ref, b_ref, o_ref):
    o_ref[...] = _mm(a_ref[...], b_ref[...]).astype(o_ref.dtype)


def matmul(a, b, out_dtype, tm, tn, name):
    m, k = a.shape
    n = b.shape[1]
    return pl.pallas_call(
        _matmul_kernel,
        grid=(m // tm, n // tn),
        in_specs=[pl.BlockSpec((tm, k), lambda i, j: (i, 0)), pl.BlockSpec((k, tn), lambda i, j: (0, j))],
        out_specs=pl.BlockSpec((tm, tn), lambda i, j: (i, j)),
        out_shape=jax.ShapeDtypeStruct((m, n), out_dtype),
        compiler_params=_params("parallel", "arbitrary"),
        name=name,
    )(a, b)


def _diff_attn_kernel(slopes_ref, lam_ref, gain_ref, q_ref, k_ref, v_ref, o_ref,
                      vt_ref, kb_ref, diag_ref, m_ref, acc_ref, *, t, lam_init):
    h = pl.program_id(1)
    qi = pl.program_id(2)
    slope = slopes_ref[h]
    q0 = pl.multiple_of(qi * t, t)
    first_map = lax.broadcasted_iota(jnp.int32, (1, LANES), 1) < DIFF_QK_DIM

    @pl.when(qi == 0)
    def _():
        vt_ref[0:HEAD_DV, :] = v_ref[...].astype(F32).T.astype(BF16)
        vt_ref[HEAD_DV:, :] = jnp.ones((SUM_ROWS, v_ref.shape[0]), BF16)
        key = lax.broadcasted_iota(jnp.int32, (t, t), 0)
        qry = lax.broadcasted_iota(jnp.int32, (t, t), 1)
        past = slope * key.astype(F32)
        visible = key < (qry // CHUNK + 1) * CHUNK
        diag = jnp.where(visible, slope * (qry - jnp.abs(qry - key)).astype(F32), NEG)
        for mi in range(2):
            kb_ref[:, mi * t:(mi + 1) * t] = past
            diag_ref[:, mi * t:(mi + 1) * t] = diag

    q = q_ref[...] * jnp.asarray(DIFF_QK_DIM ** -0.5, BF16)
    q_maps = jnp.concatenate([jnp.where(first_map, q, jnp.zeros_like(q)),
                              jnp.where(first_map, jnp.zeros_like(q), q)], axis=0)
    m_ref[...] = jnp.full(m_ref.shape, NEG, F32)
    acc_ref[...] = jnp.zeros_like(acc_ref)

    def accumulate(z, shift, vt):
        m_prev = m_ref[...]
        m_new = jnp.maximum(m_prev, jnp.max(z, axis=0, keepdims=True) + shift)
        p = jnp.exp(z - (m_new - shift))
        acc_ref[...] = jnp.exp(m_prev - m_new) * acc_ref[...] + _mm(vt, p.astype(BF16))
        m_ref[...] = m_new

    def past_tile(j, _):
        k0 = pl.multiple_of(j * t, t)
        z = _nt(k_ref[pl.ds(k0, t), :], q_maps) + kb_ref[...]
        accumulate(z, slope * (k0 - q0).astype(F32), vt_ref[:, pl.ds(k0, t)])
        return 0

    def two_past_tiles(jj, _):
        past_tile(2 * jj, 0)
        past_tile(2 * jj + 1, 0)
        return 0

    lax.fori_loop(0, qi // 2, two_past_tiles, 0)

    @pl.when(qi % 2 == 1)
    def _():
        past_tile(qi - 1, 0)

    z = _nt(k_ref[pl.ds(q0, t), :], q_maps) + diag_ref[...]
    accumulate(z, 0.0, vt_ref[:, pl.ds(q0, t)])

    lam = lam_ref[...]
    lam_full = (jnp.exp(jnp.sum(lam[0:1] * lam[1:2], axis=-1, keepdims=True))
                - jnp.exp(jnp.sum(lam[2:3] * lam[3:4], axis=-1, keepdims=True)) + lam_init)
    acc = acc_ref[...]
    ratio = acc[:HEAD_DV] / acc[HEAD_DV:HEAD_DV + 1]
    o = (ratio[:, :t] - lam_full * ratio[:, t:]).T
    o_ref[...] = (_rms(o, gain_ref[...]) * (1.0 - lam_init)).astype(o_ref.dtype)


def diff_attention(u_mm, lam, gain, layer_idx, batch, seq, tq=512):
    lam_init = 0.8 - 0.6 * math.exp(-0.3 * layer_idx)
    nq = seq // tq
    slopes = jnp.asarray([2.0 ** (-8.0 * (i + 1) / N_HEADS) for i in range(N_HEADS)], F32)
    cq, ck, cv = MM_DIFF_Q // LANES, MM_DIFF_K // LANES, MM_DIFF_V // LANES
    kern = functools.partial(_diff_attn_kernel, t=tq, lam_init=lam_init)
    return pl.pallas_call(
        kern,
        grid=(batch, N_HEADS, nq),
        in_specs=[
            pl.BlockSpec(memory_space=pltpu.SMEM),
            pl.BlockSpec((4, DIFF_QK_DIM), lambda b, h, i: (0, 0)),
            pl.BlockSpec((1, HEAD_DV), lambda b, h, i: (0, h)),
            pl.BlockSpec((tq, LANES), lambda b, h, i: (b * nq + i, cq + h)),
            pl.BlockSpec((seq, LANES), lambda b, h, i: (b, ck + h)),
            pl.BlockSpec((seq, LANES), lambda b, h, i: (b, cv + h)),
        ],
        out_specs=pl.BlockSpec((tq, HEAD_DV), lambda b, h, i: (b * nq + i, h)),
        out_shape=jax.ShapeDtypeStruct((batch * seq, BRANCH_WIDTH), BF16),
        scratch_shapes=[pltpu.VMEM((HEAD_DV + SUM_ROWS, seq), BF16),
                        pltpu.VMEM((tq, 2 * tq), F32),
                        pltpu.VMEM((tq, 2 * tq), F32),
                        pltpu.VMEM((1, 2 * tq), F32),
                        pltpu.VMEM((HEAD_DV + SUM_ROWS, 2 * tq), F32)],
        compiler_params=_params("parallel", "parallel", "arbitrary"),
        name="diff_attention",
    )(slopes, lam, gain.reshape(1, BRANCH_WIDTH), u_mm, u_mm, u_mm)


def _recurrent_mixers_kernel(
        g_ref,
        gq_ref, gk_ref, gv_ref, gr_ref, wgu_ref, bg_ref, ggain_ref,
        qk_ref, v_ref, op_ref, cw_ref, cb_ref, gb_ref, gain_ref,
        go_ref, o_ref,
        st_ref, c_ref, n_ref, m_ref, halo_ref, *, tt):
    w = BRANCH_WIDTH
    halo = halo_ref.shape[0]

    @pl.when(pl.program_id(1) == 0)
    def _():
        st_ref[...] = jnp.zeros_like(st_ref)
        c_ref[...] = jnp.zeros_like(c_ref)
        n_ref[...] = jnp.zeros_like(n_ref)
        m_ref[...] = jnp.zeros_like(m_ref)
        halo_ref[...] = jnp.zeros_like(halo_ref)

    dkh = N_HEADS * GLA_DK
    hc = N_HEADS * CHUNK
    iota = lambda shape, dim: lax.broadcasted_iota(jnp.int32, shape, dim)
    tril = (iota((CHUNK, CHUNK), 0) >= iota((CHUNK, CHUNK), 1)).astype(BF16)
    lane = iota((1, LANES), 1)
    head_of_dk_lane = iota((1, dkh), 1) // GLA_DK
    head_of_dv_lane = iota((1, w), 1) // HEAD_DV
    gla_causal = iota((CHUNK, hc), 0) >= iota((CHUNK, hc), 1) % CHUNK
    gla_state_block = iota((w, dkh), 0) // HEAD_DV == iota((w, dkh), 1) // GLA_DK
    r_i, c_i = iota((hc, hc), 0), iota((hc, hc), 1)
    ml_visible = (r_i // CHUNK == c_i // CHUNK) & (r_i % CHUNK >= c_i % CHUNK)

    def per_head_rows(a, width):
        return jnp.concatenate([a[:, h * width:(h + 1) * width] for h in range(N_HEADS)], axis=0)

    def head_blocks(a, head_of_lane):
        return jnp.concatenate([jnp.where(head_of_lane == h, a, jnp.zeros_like(a)) for h in range(N_HEADS)], axis=0)

    def gla_chunk(c):
        r0 = pl.multiple_of(c * CHUNK, CHUNK)
        rows = pl.ds(r0, CHUNK)
        pre = _mm(g_ref[rows, :].astype(BF16), wgu_ref[...]) + bg_ref[...]
        log_a = _log_sigmoid(pre) * (1.0 / GLA_GATE_NORM)
        cum = _tri_cumsum(tril, log_a)
        total = cum[CHUNK - 1:CHUNK, :]
        qf = gq_ref[rows, :].astype(F32) * (GLA_DK ** -0.5)
        kf = gk_ref[rows, :].astype(F32)
        q_dec = (qf * jnp.exp(cum)).astype(BF16)
        k_inv = (kf * jnp.exp(-cum)).astype(BF16)
        k_tail = (kf * jnp.exp(total - cum)).astype(BF16)
        v = gv_ref[rows, :]
        att = jnp.where(gla_causal, _nt(q_dec, head_blocks(k_inv, head_of_dk_lane)), 0.0)
        st = st_ref[...]
        o = _mm(att.astype(BF16), head_blocks(v, head_of_dv_lane)) + _nt(q_dec, st.astype(BF16))
        st_ref[...] = jnp.exp(total) * st + jnp.where(gla_state_block, _tn(v, k_tail), 0.0)
        for h in range(N_HEADS):
            vs = slice(h * HEAD_DV, (h + 1) * HEAD_DV)
            r = gr_ref[rows, vs]
            y = _rms(o[:, vs], ggain_ref[:, vs]) * (r * _sigmoid(r))
            go_ref[rows, vs] = y.astype(go_ref.dtype)

    def mlstm_chunk(c):
        r0 = pl.multiple_of(c * CHUNK, CHUNK)
        rows = pl.ds(r0, CHUNK)
        p0 = pl.multiple_of(jnp.maximum(r0 - halo, 0), halo)
        prev = jnp.where(c > 0, qk_ref[pl.ds(p0, halo), :], halo_ref[...])
        ext = jnp.concatenate([prev, qk_ref[rows, :]], axis=0)
        acc = cb_ref[...]
        for j in range(MLSTM_CONV):
            s0 = halo - (MLSTM_CONV - 1) + j
            acc = acc + cw_ref[j:j + 1, :] * ext[s0:s0 + CHUNK, :]
        qk = acc * _sigmoid(acc)
        gb = g_ref[rows, :] + gb_ref[...]
        log_f = _log_sigmoid(gb)
        bcum = _tri_cumsum(tril, log_f)
        x = jnp.where(lane < GATE_ML_F, gb, bcum)
        xt = x.T
        heads = range(N_HEADS)
        rep = lambda per_head: jnp.concatenate(
            [jnp.broadcast_to(a, (CHUNK, a.shape[-1])) for a in per_head], axis=0)
        bcum_c = jnp.concatenate([bcum[:, GATE_ML_F + h:GATE_ML_F + h + 1] for h in heads], axis=0)
        logi_c = jnp.concatenate([gb[:, GATE_ML_I + h:GATE_ML_I + h + 1] for h in heads], axis=0)
        r_row = jnp.concatenate([xt[GATE_ML_I + h:GATE_ML_I + h + 1, :] - xt[GATE_ML_F + h:GATE_ML_F + h + 1, :]
                                 for h in heads], axis=1)
        btot = [bcum[CHUNK - 1:CHUNK, GATE_ML_F + h:GATE_ML_F + h + 1] for h in heads]
        m_prev = [m_ref[h] for h in heads]
        tail = rep(btot) - bcum_c + logi_c
        m_loc = [jnp.max(tail[h * CHUNK:(h + 1) * CHUNK], axis=0, keepdims=True) for h in heads]
        w_tail = jnp.exp(tail - rep(m_loc))
        q_lanes = qk[:, :w].astype(BF16)
        q = per_head_rows(q_lanes, HEAD_DV)
        kf = per_head_rows(qk[:, w:], HEAD_DV) * (HEAD_DV ** -0.5)
        v = per_head_rows(v_ref[rows, :], HEAD_DV)
        kw = kf * w_tail
        dlog = jnp.where(ml_visible, bcum_c + r_row, NEG)
        inter_log = bcum_c + rep(m_prev)
        m_t = jnp.maximum(jnp.max(dlog, axis=-1, keepdims=True), inter_log)
        a_mat = jnp.exp(dlog - m_t) * _nt(q, kf.astype(BF16))
        inter_w = jnp.exp(inter_log - m_t)
        c_prev = c_ref[...]
        n_prev = [n_ref[h] for h in heads]
        num = _mm(a_mat.astype(BF16), v) + inter_w * _mm(head_blocks(q_lanes, head_of_dv_lane), c_prev.astype(BF16))
        qn = jnp.sum(q.astype(F32) * rep(n_prev), axis=-1, keepdims=True)
        den = jnp.sum(a_mat, axis=-1, keepdims=True) + inter_w * qn
        hh = num / jnp.maximum(jnp.abs(den), jnp.exp(-m_t))
        for h in heads:
            hs = slice(h * HEAD_DV, (h + 1) * HEAD_DV)
            y = _rms(_sigmoid(op_ref[rows, hs]) * hh[h * CHUNK:(h + 1) * CHUNK], gain_ref[:, hs])
            o_ref[rows, hs] = y.astype(o_ref.dtype)
        kw_lanes = jnp.concatenate([kw[h * CHUNK:(h + 1) * CHUNK] for h in heads], axis=1)
        d_c = _tn(head_blocks(kw_lanes.astype(BF16), head_of_dv_lane), v)
        rep_d = lambda per_head: jnp.concatenate(
            [jnp.broadcast_to(a, (HEAD_DV, 1)) for a in per_head], axis=0)
        m_new = [jnp.maximum(btot[h] + m_prev[h], m_loc[h]) for h in heads]
        a = [jnp.exp(btot[h] + m_prev[h] - m_new[h]) for h in heads]
        bb = [jnp.exp(m_loc[h] - m_new[h]) for h in heads]
        c_ref[...] = rep_d(a) * c_prev + rep_d(bb) * d_c
        for h in heads:
            d_n = jnp.sum(kw[h * CHUNK:(h + 1) * CHUNK], axis=0, keepdims=True)
            n_ref[h] = a[h] * n_prev[h] + bb[h] * d_n
            m_ref[h] = m_new[h]

    def chunk(c, _):
        gla_chunk(c)
        mlstm_chunk(c)
        return 0

    lax.fori_loop(0, tt // CHUNK, chunk, 0, unroll=4)
    halo_ref[...] = qk_ref[tt - halo:tt, :]


def recurrent_mixers(u_mm, u_fx, gla_w_gate_up, gla_b_gate, gla_gain, conv_w, conv_b, b_i, b_f, ml_gain,
                     batch, seq, tt=1024):
    w = BRANCH_WIDTH
    dkh = N_HEADS * GLA_DK
    tt = min(tt, seq)
    nt = seq // tt
    wgu = (jnp.zeros((LANES, dkh), F32).at[GATE_GLA_LOW:GATE_GLA_LOW + GLA_GATE_RANK].set(gla_w_gate_up)
           .astype(BF16))
    gate_bias = (jnp.zeros((1, LANES), F32).at[0, GATE_ML_I:GATE_ML_I + N_HEADS].set(b_i)
                 .at[0, GATE_ML_F:GATE_ML_F + N_HEADS].set(b_f))
    const = lambda b, t: (0, 0)
    rows = lambda width, col0: pl.BlockSpec((tt, width), lambda b, t: (b * nt + t, col0 // width))
    out = jax.ShapeDtypeStruct((batch * seq, w), BF16)
    return pl.pallas_call(
        functools.partial(_recurrent_mixers_kernel, tt=tt),
        grid=(batch, nt),
        in_specs=[
            rows(LANES, FX_GATES),
            rows(dkh, MM_GLA_Q), rows(dkh, MM_GLA_K), rows(w, MM_GLA_V), rows(w, FX_GLA_R),
            pl.BlockSpec((LANES, dkh), const), pl.BlockSpec((1, dkh), const), pl.BlockSpec((1, w), const),
            rows(2 * w, FX_ML_QK), rows(w, MM_ML_V), rows(w, FX_ML_O),
            pl.BlockSpec((MLSTM_CONV, 2 * w), const), pl.BlockSpec((1, 2 * w), const),
            pl.BlockSpec((1, LANES), const), pl.BlockSpec((1, w), const),
        ],
        out_specs=[rows(w, 0), rows(w, 0)],
        out_shape=[out, out],
        scratch_shapes=[pltpu.VMEM((w, dkh), F32),
                        pltpu.VMEM((N_HEADS * HEAD_DV, HEAD_DV), F32),
                        pltpu.VMEM((N_HEADS, 1, HEAD_DV), F32),
                        pltpu.VMEM((N_HEADS, 1, 1), F32),
                        pltpu.VMEM((SUBLANES, 2 * w), F32)],
        compiler_params=_params("parallel", "arbitrary"),
        name="recurrent_mixers",
    )(u_fx,
      u_mm, u_mm, u_mm, u_fx, wgu, gla_b_gate.reshape(1, dkh), gla_gain.reshape(1, w),
      u_fx, u_mm, u_fx, conv_w, conv_b.reshape(1, 2 * w), gate_bias, ml_gain.reshape(1, w))


BAND_Q_CHUNKS = 4
BAND_TQ = BAND_Q_CHUNKS * CHUNK
BAND_TK = (CA_LEFT_CHUNKS + BAND_Q_CHUNKS) * CHUNK
BAND_PAD = CA_LEFT_CHUNKS * CHUNK


def _band_bias_kernel(table_ref, o_ref):
    l = pl.program_id(0)
    h = pl.program_id(1)
    row = lax.broadcasted_iota(jnp.int32, (BAND_TQ, BAND_TK), 0)
    col = lax.broadcasted_iota(jnp.int32, (BAND_TQ, BAND_TK), 1)
    idx = jnp.clip(row - col + BAND_PAD, -REL_CLIP, REL_CLIP) + REL_CLIP
    delta = row // CHUNK + CA_LEFT_CHUNKS - col // CHUNK
    in_band = (delta >= 0) & (delta <= CA_LEFT_CHUNKS)

    table = table_ref[0, 0]
    tiles = []
    for c0 in range(0, BAND_TK, LANES):
        i = idx[:, c0:c0 + LANES]
        val = jnp.zeros(i.shape, F32)
        for ch in range(table.shape[0]):
            src = jnp.broadcast_to(table[ch:ch + 1, :], i.shape)
            looked_up = jnp.take_along_axis(src, i & (LANES - 1), axis=1)
            val = jnp.where(i // LANES == ch, looked_up, val)
        tiles.append(val)
    o_ref[0, 0] = jnp.where(in_band, jnp.concatenate(tiles, axis=1), NEG)


def band_bias(rel_bias):
    depth, heads, n = rel_bias.shape
    chunks = -(-n // LANES)
    table = jnp.pad(rel_bias, ((0, 0), (0, 0), (0, chunks * LANES - n))).reshape(depth, heads, chunks, LANES)
    return pl.pallas_call(
        _band_bias_kernel,
        grid=(depth, N_HEADS),
        in_specs=[pl.BlockSpec((1, 1, chunks, LANES), lambda l, h: (l, h, 0, 0))],
        out_specs=pl.BlockSpec((1, 1, BAND_TQ, BAND_TK), lambda l, h: (l, h, 0, 0)),
        out_shape=jax.ShapeDtypeStruct((depth, N_HEADS, BAND_TQ, BAND_TK), F32),
        compiler_params=_params("parallel", "parallel"),
        name="band_bias",
    )(table)


def _band_attn_kernel(bias_ref, q_ref, k_ref, v_ref, o_ref, kp_ref, vp_ref, *, seq):
    kp_ref[0:BAND_PAD, :] = jnp.zeros((BAND_PAD, HEAD_DV), BF16)
    vp_ref[0:BAND_PAD, 0:HEAD_DV] = jnp.zeros((BAND_PAD, HEAD_DV), BF16)
    kp_ref[BAND_PAD:BAND_PAD + seq, :] = k_ref[...]
    vp_ref[BAND_PAD:BAND_PAD + seq, 0:HEAD_DV] = v_ref[...]
    vp_ref[:, HEAD_DV:] = jnp.ones((BAND_PAD + seq, HEAD_DV), BF16)
    scale = HEAD_DV ** -0.5

    def tile(t, starts_in_pad):
        r0 = t * BAND_TQ
        r0 = r0 if starts_in_pad else pl.multiple_of(r0, BAND_TQ)
        q = q_ref[pl.ds(r0, BAND_TQ), :]
        kb = kp_ref[pl.ds(r0, BAND_TK), :]
        vb = vp_ref[pl.ds(r0, BAND_TK), :]
        s = _nt(q, kb) * scale + bias_ref[0, 0]
        if starts_in_pad:
            col = lax.broadcasted_iota(jnp.int32, (BAND_TQ, BAND_TK), 1)
            s = jnp.where(col + r0 >= BAND_PAD, s, NEG)
        p = jnp.exp(s - jnp.max(s, axis=-1, keepdims=True))
        ov = _mm(p.astype(BF16), vb)
        o_ref[pl.ds(r0, BAND_TQ), :] = (ov[:, :HEAD_DV] / ov[:, HEAD_DV:]).astype(o_ref.dtype)

    n_pad_tiles = BAND_PAD // BAND_TQ
    for t in range(n_pad_tiles):
        tile(t, True)

    def body(t, _):
        tile(t, False)
        return 0

    lax.fori_loop(n_pad_tiles, seq // BAND_TQ, body, 0, unroll=7)


def band_attention(u_mm, bias, layer, batch, seq):
    cq, ck, cv = MM_CA_Q // LANES, MM_CA_K // LANES, MM_CA_V // LANES
    kern = functools.partial(_band_attn_kernel, seq=seq)
    return pl.pallas_call(
        kern,
        grid=(batch, N_HEADS),
        in_specs=[
            pl.BlockSpec((1, 1, BAND_TQ, BAND_TK), lambda b, h: (layer, h, 0, 0)),
            pl.BlockSpec((seq, LANES), lambda b, h: (b, cq + h)),
            pl.BlockSpec((seq, LANES), lambda b, h: (b, ck + h)),
            pl.BlockSpec((seq, LANES), lambda b, h: (b, cv + h)),
        ],
        out_specs=pl.BlockSpec((seq, HEAD_DV), lambda b, h: (b, h)),
        out_shape=jax.ShapeDtypeStruct((batch * seq, BRANCH_WIDTH), BF16),
        scratch_shapes=[pltpu.VMEM((BAND_PAD + seq, HEAD_DV), BF16),
                        pltpu.VMEM((BAND_PAD + seq, 2 * HEAD_DV), BF16)],
        compiler_params=_params("parallel", "parallel"),
        name="band_attention",
    )(bias, u_mm, u_mm, u_mm)


def _gated_merge_kernel(h_ref, ba_ref, bb_ref, bc_ref, bd_ref, wg_ref, bg_ref, wb_ref, o_ref):
    h = h_ref[...]
    merged = None
    for bi, br_ref in enumerate((ba_ref, bb_ref, bc_ref, bd_ref)):
        gate = _sigmoid(_mm(h, wg_ref[bi]) + bg_ref[bi])
        term = gate * _mm(br_ref[...], wb_ref[bi])
        merged = term if merged is None else merged + term
    o_ref[...] = merged.astype(o_ref.dtype)


def gated_merge(h, branches, w_gate, b_gate, w_branch, layer, tm=1024, tn=512):
    m, d = h.shape
    w = BRANCH_WIDTH
    tm = min(tm, m)
    row_d = pl.BlockSpec((tm, d), lambda i, j: (i, 0))
    row_w = pl.BlockSpec((tm, w), lambda i, j: (i, 0))
    return pl.pallas_call(
        _gated_merge_kernel,
        grid=(m // tm, d // tn),
        in_specs=[
            row_d, row_w, row_w, row_w, row_w,
            pl.BlockSpec((None, 4, d, tn), lambda i, j: (layer, 0, 0, j)),
            pl.BlockSpec((None, 4, 1, tn), lambda i, j: (layer, 0, 0, j)),
            pl.BlockSpec((None, 4, w, tn), lambda i, j: (layer, 0, 0, j)),
        ],
        out_specs=pl.BlockSpec((tm, tn), lambda i, j: (i, j)),
        out_shape=jax.ShapeDtypeStruct((m, d), BF16),
        compiler_params=_params("parallel", "arbitrary"),
        name="gated_merge",
    )(h, *branches, w_gate, b_gate.reshape(-1, 4, 1, d), w_branch)


def _out_proj_kernel(a_ref, x_ref, wo_ref, gpost_ref, gnext_ref, xo_ref, ho_ref, *, sub):
    for r in range(a_ref.shape[0] // sub):
        rows = slice(r * sub, (r + 1) * sub)
        x_new = x_ref[rows, :] + _rms(_mm(a_ref[rows, :], wo_ref[...]), gpost_ref[...])
        xo_ref[rows, :] = x_new
        ho_ref[rows, :] = _rms(x_new, gnext_ref[...]).astype(ho_ref.dtype)


def out_proj(a, x, w_out, layer, g_post, g_next, tm=512, sub=128):
    m, d = x.shape
    row_d = pl.BlockSpec((tm, d), lambda i: (i, 0))
    vec_d = pl.BlockSpec((1, d), lambda i: (0, 0))
    return pl.pallas_call(
        functools.partial(_out_proj_kernel, sub=min(sub, tm)),
        grid=(m // tm,),
        in_specs=[row_d, row_d, pl.BlockSpec((None, d, d), lambda i: (layer, 0, 0)), vec_d, vec_d],
        out_specs=[row_d, row_d],
        out_shape=[jax.ShapeDtypeStruct((m, d), F32), jax.ShapeDtypeStruct((m, d), BF16)],
        compiler_params=_params("parallel"),
        name="out_proj",
    )(a, x, w_out, g_post.reshape(1, d), g_next.reshape(1, d))


def _ffn_kernel(h_ref, x_hbm, wu_ref, wd_ref, gpost_ref, *rest, emit_next, sub):
    if emit_next:
        gnext_ref, xo_ref, ho_ref, x_ref, x_sem = rest
    else:
        xo_ref, x_ref, x_sem = rest
    i = pl.program_id(0)
    j = pl.program_id(1)
    tm = x_ref.shape[0]

    def x_copy():
        return pltpu.make_async_copy(x_hbm.at[pl.ds(pl.multiple_of(i * tm, tm), tm)], x_ref, x_sem)

    @pl.when(j == 0)
    def _():
        x_copy().start()
        xo_ref[...] = jnp.zeros_like(xo_ref)

    for s in range(wu_ref.shape[1] // sub):
        cols = slice(s * sub, (s + 1) * sub)
        act = jnp.square(jnp.maximum(_mm(h_ref[...], wu_ref[:, cols]), 0.0))
        xo_ref[...] += _mm(act.astype(BF16), wd_ref[cols, :])

    @pl.when(j == pl.num_programs(1) - 1)
    def _():
        x_copy().wait()
        x_new = x_ref[...] + _rms(xo_ref[...], gpost_ref[...])
        xo_ref[...] = x_new
        if emit_next:
            ho_ref[...] = _rms(x_new, gnext_ref[...]).astype(ho_ref.dtype)


def ffn(h, x, w_up, w_down, layer, g_post, g_next=None, tm=1024, tf=1024, sub=512):
    m, d = x.shape
    f = w_up.shape[2]
    tm = min(tm, m)
    emit_next = g_next is not None
    row_d = pl.BlockSpec((tm, d), lambda i, j: (i, 0))
    vec_d = pl.BlockSpec((1, d), lambda i, j: (0, 0))
    in_specs = [row_d, pl.BlockSpec(memory_space=pl.ANY), pl.BlockSpec((None, d, tf), lambda i, j: (layer, 0, j)),
                pl.BlockSpec((None, tf, d), lambda i, j: (layer, j, 0)), vec_d]
    args = [h, x, w_up, w_down, g_post.reshape(1, d)]
    out_specs = [row_d]
    out_shape = [jax.ShapeDtypeStruct((m, d), F32)]
    if emit_next:
        in_specs.append(vec_d)
        args.append(g_next.reshape(1, d))
        out_specs.append(row_d)
        out_shape.append(jax.ShapeDtypeStruct((m, d), BF16))
    return pl.pallas_call(
        functools.partial(_ffn_kernel, emit_next=emit_next, sub=sub),
        grid=(m // tm, f // tf),
        in_specs=in_specs,
        out_specs=out_specs,
        out_shape=out_shape,
        scratch_shapes=[pltpu.VMEM((tm, d), F32), pltpu.SemaphoreType.DMA(())],
        compiler_params=pltpu.CompilerParams(dimension_semantics=("parallel", "arbitrary"),
                                             vmem_limit_bytes=FFN_VMEM_LIMIT),
        name="ffn",
    )(*args)


def _pack_w_in(w_in_l):
    w = BRANCH_WIDTH
    gla0 = 3 * w
    ml0 = gla0 + 2 * N_HEADS * GLA_DK + w + GLA_GATE_RANK + w
    ca0 = ml0 + 4 * w + 2 * N_HEADS
    c = lambda a, b: w_in_l[:, a:b]
    g_low0 = gla0 + 2 * N_HEADS * GLA_DK + w
    gates = jnp.concatenate([c(g_low0, g_low0 + GLA_GATE_RANK), c(ml0 + 4 * w, ml0 + 4 * w + 2 * N_HEADS)], axis=1)
    gates = jnp.pad(gates, ((0, 0), (0, LANES - gates.shape[1])))
    w_mm = jnp.concatenate([c(0, 3 * w), c(gla0, gla0 + 2 * N_HEADS * GLA_DK + w),
                            c(ml0 + 2 * w, ml0 + 3 * w), c(ca0, ca0 + 3 * w)], axis=1)
    w_fx = jnp.concatenate([c(ml0, ml0 + 2 * w), c(g_low0 + GLA_GATE_RANK, g_low0 + GLA_GATE_RANK + w),
                            c(ml0 + 3 * w, ml0 + 4 * w), gates], axis=1)
    return w_mm, w_fx


def kernel(x, norm_mix_pre, norm_mix_post, w_in, diff_lambda, diff_norm, gla_w_gate_up, gla_b_gate, gla_norm,
           mlstm_conv_w, mlstm_conv_b, mlstm_b_i, mlstm_b_f, mlstm_norm, rel_bias, w_branch, w_gate, b_gate,
           w_out, norm_ffn_pre, norm_ffn_post, w_up, w_down):
    batch, seq, d = x.shape
    depth = w_in.shape[0]
    m = batch * seq
    xf = x.reshape(m, d)
    bias_tiles = band_bias(rel_bias)
    w_gate, w_branch, w_out, w_up, w_down = (a.astype(BF16) for a in (w_gate, w_branch, w_out, w_up, w_down))
    h = rmsnorm_cast(xf, norm_mix_pre[0])
    for l in range(depth):
        w_mm, w_fx = _pack_w_in(w_in[l].astype(BF16))
        u_mm = matmul(h, w_mm, BF16, tm=min(1024, m), tn=MM_COLS // 2, name="in_proj_mm")
        u_fx = matmul(h, w_fx, F32, tm=min(1024, m), tn=FX_COLS, name="in_proj_fx")
        o_gla, o_mlstm = recurrent_mixers(u_mm, u_fx, gla_w_gate_up[l], gla_b_gate[l], gla_norm[l], mlstm_conv_w[l],
                                          mlstm_conv_b[l], mlstm_b_i[l], mlstm_b_f[l], mlstm_norm[l], batch, seq)
        branches = (
            diff_attention(u_mm, diff_lambda[l], diff_norm[l], l, batch, seq),
            o_gla,
            o_mlstm,
            band_attention(u_mm, bias_tiles, l, batch, seq),
        )
        merged = gated_merge(h, branches, w_gate, b_gate, w_branch, l)
        xf, h2 = out_proj(merged, xf, w_out, l, norm_mix_post[l], norm_ffn_pre[l])
        g_next = norm_mix_pre[l + 1] if l + 1 < depth else None
        outs = ffn(h2, xf, w_up, w_down, l, norm_ffn_post[l], g_next)
        xf, h = outs if g_next is not None else (outs[0], None)
    return xf.reshape(batch, seq, d)
```

```python
import functools
import math

import jax
import jax.numpy as jnp
from jax import lax
from jax.experimental import pallas as pl
from jax.experimental.pallas import tpu as pltpu

F32 = jnp.float32
BF16 = jnp.bfloat16

CHUNK = 64
BRANCH_WIDTH = 512
N_HEADS = 4
HEAD_DV = 128
DIFF_QK_DIM = 64
GLA_DK = 64
GLA_GATE_RANK = 16
GLA_GATE_NORM = 16.0
MLSTM_CONV = 4
CA_LEFT_CHUNKS = 8
REL_CLIP = 2 * CHUNK
EPS = 1e-6
NEG = -1e30

LANES = 128
SUBLANES = 8
VMEM_LIMIT = 56 * 1024 * 1024
FFN_VMEM_LIMIT = 60 * 1024 * 1024

MM_DIFF_Q, MM_DIFF_K, MM_DIFF_V = 0, 512, 1024
MM_GLA_Q, MM_GLA_K, MM_GLA_V = 1536, 1792, 2048
MM_ML_V = 2560
MM_CA_Q, MM_CA_K, MM_CA_V = 3072, 3584, 4096
MM_COLS = 4608
FX_ML_QK, FX_GLA_R, FX_ML_O, FX_GATES = 0, 1024, 1536, 2048
FX_COLS = 2176
GATE_GLA_LOW, GATE_ML_I, GATE_ML_F = 0, 16, 20

SUM_ROWS = 16

NT_DIMS = (((1,), (1,)), ((), ()))
TN_DIMS = (((0,), (0,)), ((), ()))


def _nt(a, b):
    return lax.dot_general(a, b, NT_DIMS, preferred_element_type=F32)


def _tn(a, b):
    return lax.dot_general(a, b, TN_DIMS, preferred_element_type=F32)


def _mm(a, b):
    return jnp.dot(a, b, preferred_element_type=F32)


def _tri_cumsum(tril, x):
    hi = x.astype(BF16)
    rest = x - hi.astype(F32)
    mid = rest.astype(BF16)
    lo = (rest - mid.astype(F32)).astype(BF16)
    n = x.shape[1]
    parts = _mm(tril, jnp.concatenate([hi, mid, lo], axis=1))
    return parts[:, :n] + parts[:, n:2 * n] + parts[:, 2 * n:]


def _rms(x, gain):
    return x * lax.rsqrt(jnp.mean(x * x, axis=-1, keepdims=True) + EPS) * gain


def _log_sigmoid(x):
    return jnp.minimum(x, 0.0) - jnp.log1p(jnp.exp(-jnp.abs(x)))


def _sigmoid(x):
    return 0.5 * jnp.tanh(0.5 * x) + 0.5


def _params(*sem):
    return pltpu.CompilerParams(dimension_semantics=sem, vmem_limit_bytes=VMEM_LIMIT)


def _rmsnorm_cast_kernel(x_ref, g_ref, o_ref):
    o_ref[...] = _rms(x_ref[...], g_ref[...]).astype(o_ref.dtype)


def rmsnorm_cast(x, g, tm=1024):
    m, d = x.shape
    tm = min(tm, m)
    return pl.pallas_call(
        _rmsnorm_cast_kernel,
        grid=(m // tm,),
        in_specs=[pl.BlockSpec((tm, d), lambda i: (i, 0)), pl.BlockSpec((1, d), lambda i: (0, 0))],
        out_specs=pl.BlockSpec((tm, d), lambda i: (i, 0)),
        out_shape=jax.ShapeDtypeStruct((m, d), BF16),
        compiler_params=_params("parallel"),
        name="rmsnorm_cast",
    )(x, g.reshape(1, d))


def _matmul_kernel(a_ref, b_ref, o_ref):
    o_ref[...] = _mm(a_ref[...], b_ref[...]).astype(o_ref.dtype)


def matmul(a, b, out_dtype, tm, tn, name):
    m, k = a.shape
    n = b.shape[1]
    return pl.pallas_call(
        _matmul_kernel,
        grid=(m // tm, n // tn),
        in_specs=[pl.BlockSpec((tm, k), lambda i, j: (i, 0)), pl.BlockSpec((k, tn), lambda i, j: (0, j))],
        out_specs=pl.BlockSpec((tm, tn), lambda i, j: (i, j)),
        out_shape=jax.ShapeDtypeStruct((m, n), out_dtype),
        compiler_params=_params("parallel", "arbitrary"),
        name=name,
    )(a, b)


def _diff_attn_kernel(slopes_ref, lam_ref, gain_ref, q_ref, k_ref, v_ref, o_ref,
                      vt_ref, kb_ref, diag_ref, m_ref, acc_ref, *, t, lam_init):
    h = pl.program_id(1)
    qi = pl.program_id(2)
    slope = slopes_ref[h]
    q0 = pl.multiple_of(qi * t, t)
    first_map = lax.broadcasted_iota(jnp.int32, (1, LANES), 1) < DIFF_QK_DIM

    @pl.when(qi == 0)
    def _():
        vt_ref[0:HEAD_DV, :] = v_ref[...].astype(F32).T.astype(BF16)
        vt_ref[HEAD_DV:, :] = jnp.ones((SUM_ROWS, v_ref.shape[0]), BF16)
        key = lax.broadcasted_iota(jnp.int32, (t, t), 0)
        qry = lax.broadcasted_iota(jnp.int32, (t, t), 1)
        past = slope * key.astype(F32)
        visible = key < (qry // CHUNK + 1) * CHUNK
        diag = jnp.where(visible, slope * (qry - jnp.abs(qry - key)).astype(F32), NEG)
        for mi in range(2):
            kb_ref[:, mi * t:(mi + 1) * t] = past
            diag_ref[:, mi * t:(mi + 1) * t] = diag

    q = q_ref[...] * jnp.asarray(DIFF_QK_DIM ** -0.5, BF16)
    q_maps = jnp.concatenate([jnp.where(first_map, q, jnp.zeros_like(q)),
                              jnp.where(first_map, jnp.zeros_like(q), q)], axis=0)
    m_ref[...] = jnp.full(m_ref.shape, NEG, F32)
    acc_ref[...] = jnp.zeros_like(acc_ref)

    def accumulate(z, shift, vt):
        m_prev = m_ref[...]
        m_new = jnp.maximum(m_prev, jnp.max(z, axis=0, keepdims=True) + shift)
        p = jnp.exp(z - (m_new - shift))
        acc_ref[...] = jnp.exp(m_prev - m_new) * acc_ref[...] + _mm(vt, p.astype(BF16))
        m_ref[...] = m_new

    def past_tile(j, _):
        k0 = pl.multiple_of(j * t, t)
        z = _nt(k_ref[pl.ds(k0, t), :], q_maps) + kb_ref[...]
        accumulate(z, slope * (k0 - q0).astype(F32), vt_ref[:, pl.ds(k0, t)])
        return 0

    def two_past_tiles(jj, _):
        past_tile(2 * jj, 0)
        past_tile(2 * jj + 1, 0)
        return 0

    lax.fori_loop(0, qi // 2, two_past_tiles, 0)

    @pl.when(qi % 2 == 1)
    def _():
        past_tile(qi - 1, 0)

    z = _nt(k_ref[pl.ds(q0, t), :], q_maps) + diag_ref[...]
    accumulate(z, 0.0, vt_ref[:, pl.ds(q0, t)])

    lam = lam_ref[...]
    lam_full = (jnp.exp(jnp.sum(lam[0:1] * lam[1:2], axis=-1, keepdims=True))
                - jnp.exp(jnp.sum(lam[2:3] * lam[3:4], axis=-1, keepdims=True)) + lam_init)
    acc = acc_ref[...]
    ratio = acc[:HEAD_DV] / acc[HEAD_DV:HEAD_DV + 1]
    o = (ratio[:, :t] - lam_full * ratio[:, t:]).T
    o_ref[...] = (_rms(o, gain_ref[...]) * (1.0 - lam_init)).astype(o_ref.dtype)


def diff_attention(u_mm, lam, gain, layer_idx, batch, seq, tq=512):
    lam_init = 0.8 - 0.6 * math.exp(-0.3 * layer_idx)
    nq = seq // tq
    slopes = jnp.asarray([2.0 ** (-8.0 * (i + 1) / N_HEADS) for i in range(N_HEADS)], F32)
    cq, ck, cv = MM_DIFF_Q // LANES, MM_DIFF_K // LANES, MM_DIFF_V // LANES
    kern = functools.partial(_diff_attn_kernel, t=tq, lam_init=lam_init)
    return pl.pallas_call(
        kern,
        grid=(batch, N_HEADS, nq),
        in_specs=[
            pl.BlockSpec(memory_space=pltpu.SMEM),
            pl.BlockSpec((4, DIFF_QK_DIM), lambda b, h, i: (0, 0)),
            pl.BlockSpec((1, HEAD_DV), lambda b, h, i: (0, h)),
            pl.BlockSpec((tq, LANES), lambda b, h, i: (b * nq + i, cq + h)),
            pl.BlockSpec((seq, LANES), lambda b, h, i: (b, ck + h)),
            pl.BlockSpec((seq, LANES), lambda b, h, i: (b, cv + h)),
        ],
        out_specs=pl.BlockSpec((tq, HEAD_DV), lambda b, h, i: (b * nq + i, h)),
        out_shape=jax.ShapeDtypeStruct((batch * seq, BRANCH_WIDTH), BF16),
        scratch_shapes=[pltpu.VMEM((HEAD_DV + SUM_ROWS, seq), BF16),
                        pltpu.VMEM((tq, 2 * tq), F32),
                        pltpu.VMEM((tq, 2 * tq), F32),
                        pltpu.VMEM((1, 2 * tq), F32),
                        pltpu.VMEM((HEAD_DV + SUM_ROWS, 2 * tq), F32)],
        compiler_params=_params("parallel", "parallel", "arbitrary"),
        name="diff_attention",
    )(slopes, lam, gain.reshape(1, BRANCH_WIDTH), u_mm, u_mm, u_mm)


def _recurrent_mixers_kernel(
        g_ref,
        gq_ref, gk_ref, gv_ref, gr_ref, wgu_ref, bg_ref, ggain_ref,
        qk_ref, v_ref, op_ref, cw_ref, cb_ref, gb_ref, gain_ref,
        go_ref, o_ref,
        st_ref, c_ref, n_ref, m_ref, halo_ref, *, tt):
    w = BRANCH_WIDTH
    halo = halo_ref.shape[0]

    @pl.when(pl.program_id(1) == 0)
    def _():
        st_ref[...] = jnp.zeros_like(st_ref)
        c_ref[...] = jnp.zeros_like(c_ref)
        n_ref[...] = jnp.zeros_like(n_ref)
        m_ref[...] = jnp.zeros_like(m_ref)
        halo_ref[...] = jnp.zeros_like(halo_ref)

    dkh = N_HEADS * GLA_DK
    hc = N_HEADS * CHUNK
    iota = lambda shape, dim: lax.broadcasted_iota(jnp.int32, shape, dim)
    tril = (iota((CHUNK, CHUNK), 0) >= iota((CHUNK, CHUNK), 1)).astype(BF16)
    lane = iota((1, LANES), 1)
    head_of_dk_lane = iota((1, dkh), 1) // GLA_DK
    head_of_dv_lane = iota((1, w), 1) // HEAD_DV
    gla_causal = iota((CHUNK, hc), 0) >= iota((CHUNK, hc), 1) % CHUNK
    gla_state_block = iota((w, dkh), 0) // HEAD_DV == iota((w, dkh), 1) // GLA_DK
    r_i, c_i = iota((hc, hc), 0), iota((hc, hc), 1)
    ml_visible = (r_i // CHUNK == c_i // CHUNK) & (r_i % CHUNK >= c_i % CHUNK)

    def per_head_rows(a, width):
        return jnp.concatenate([a[:, h * width:(h + 1) * width] for h in range(N_HEADS)], axis=0)

    def head_blocks(a, head_of_lane):
        return jnp.concatenate([jnp.where(head_of_lane == h, a, jnp.zeros_like(a)) for h in range(N_HEADS)], axis=0)

    def gla_chunk(c):
        r0 = pl.multiple_of(c * CHUNK, CHUNK)
        rows = pl.ds(r0, CHUNK)
        pre = _mm(g_ref[rows, :].astype(BF16), wgu_ref[...]) + bg_ref[...]
        log_a = _log_sigmoid(pre) * (1.0 / GLA_GATE_NORM)
        cum = _tri_cumsum(tril, log_a)
        total = cum[CHUNK - 1:CHUNK, :]
        qf = gq_ref[rows, :].astype(F32) * (GLA_DK ** -0.5)
        kf = gk_ref[rows, :].astype(F32)
        q_dec = (qf * jnp.exp(cum)).astype(BF16)
        k_inv = (kf * jnp.exp(-cum)).astype(BF16)
        k_tail = (kf * jnp.exp(total - cum)).astype(BF16)
        v = gv_ref[rows, :]
        att = jnp.where(gla_causal, _nt(q_dec, head_blocks(k_inv, head_of_dk_lane)), 0.0)
        st = st_ref[...]
        o = _mm(att.astype(BF16), head_blocks(v, head_of_dv_lane)) + _nt(q_dec, st.astype(BF16))
        st_ref[...] = jnp.exp(total) * st + jnp.where(gla_state_block, _tn(v, k_tail), 0.0)
        for h in range(N_HEADS):
            vs = slice(h * HEAD_DV, (h + 1) * HEAD_DV)
            r = gr_ref[rows, vs]
            y = _rms(o[:, vs], ggain_ref[:, vs]) * (r * _sigmoid(r))
            go_ref[rows, vs] = y.astype(go_ref.dtype)

    def mlstm_chunk(c):
        r0 = pl.multiple_of(c * CHUNK, CHUNK)
        rows = pl.ds(r0, CHUNK)
        p0 = pl.multiple_of(jnp.maximum(r0 - halo, 0), halo)
        prev = jnp.where(c > 0, qk_ref[pl.ds(p0, halo), :], halo_ref[...])
        ext = jnp.concatenate([prev, qk_ref[rows, :]], axis=0)
        acc = cb_ref[...]
        for j in range(MLSTM_CONV):
            s0 = halo - (MLSTM_CONV - 1) + j
            acc = acc + cw_ref[j:j + 1, :] * ext[s0:s0 + CHUNK, :]
        qk = acc * _sigmoid(acc)
        gb = g_ref[rows, :] + gb_ref[...]
        log_f = _log_sigmoid(gb)
        bcum = _tri_cumsum(tril, log_f)
        x = jnp.where(lane < GATE_ML_F, gb, bcum)
        xt = x.T
        heads = range(N_HEADS)
        rep = lambda per_head: jnp.concatenate(
            [jnp.broadcast_to(a, (CHUNK, a.shape[-1])) for a in per_head], axis=0)
        bcum_c = jnp.concatenate([bcum[:, GATE_ML_F + h:GATE_ML_F + h + 1] for h in heads], axis=0)
        logi_c = jnp.concatenate([gb[:, GATE_ML_I + h:GATE_ML_I + h + 1] for h in heads], axis=0)
        r_row = jnp.concatenate([xt[GATE_ML_I + h:GATE_ML_I + h + 1, :] - xt[GATE_ML_F + h:GATE_ML_F + h + 1, :]
                                 for h in heads], axis=1)
        btot = [bcum[CHUNK - 1:CHUNK, GATE_ML_F + h:GATE_ML_F + h + 1] for h in heads]
        m_prev = [m_ref[h] for h in heads]
        tail = rep(btot) - bcum_c + logi_c
        m_loc = [jnp.max(tail[h * CHUNK:(h + 1) * CHUNK], axis=0, keepdims=True) for h in heads]
        w_tail = jnp.exp(tail - rep(m_loc))
        q_lanes = qk[:, :w].astype(BF16)
        q = per_head_rows(q_lanes, HEAD_DV)
        kf = per_head_rows(qk[:, w:], HEAD_DV) * (HEAD_DV ** -0.5)
        v = per_head_rows(v_ref[rows, :], HEAD_DV)
        kw = kf * w_tail
        dlog = jnp.where(ml_visible, bcum_c + r_row, NEG)
        inter_log = bcum_c + rep(m_prev)
        m_t = jnp.maximum(jnp.max(dlog, axis=-1, keepdims=True), inter_log)
        a_mat = jnp.exp(dlog - m_t) * _nt(q, kf.astype(BF16))
        inter_w = jnp.exp(inter_log - m_t)
        c_prev = c_ref[...]
        n_prev = [n_ref[h] for h in heads]
        num = _mm(a_mat.astype(BF16), v) + inter_w * _mm(head_blocks(q_lanes, head_of_dv_lane), c_prev.astype(BF16))
        qn = jnp.sum(q.astype(F32) * rep(n_prev), axis=-1, keepdims=True)
        den = jnp.sum(a_mat, axis=-1, keepdims=True) + inter_w * qn
        hh = num / jnp.maximum(jnp.abs(den), jnp.exp(-m_t))
        for h in heads:
            hs = slice(h * HEAD_DV, (h + 1) * HEAD_DV)
            y = _rms(_sigmoid(op_ref[rows, hs]) * hh[h * CHUNK:(h + 1) * CHUNK], gain_ref[:, hs])
            o_ref[rows, hs] = y.astype(o_ref.dtype)
        kw_lanes = jnp.concatenate([kw[h * CHUNK:(h + 1) * CHUNK] for h in heads], axis=1)
        d_c = _tn(head_blocks(kw_lanes.astype(BF16), head_of_dv_lane), v)
        rep_d = lambda per_head: jnp.concatenate(
            [jnp.broadcast_to(a, (HEAD_DV, 1)) for a in per_head], axis=0)
        m_new = [jnp.maximum(btot[h] + m_prev[h], m_loc[h]) for h in heads]
        a = [jnp.exp(btot[h] + m_prev[h] - m_new[h]) for h in heads]
        bb = [jnp.exp(m_loc[h] - m_new[h]) for h in heads]
        c_ref[...] = rep_d(a) * c_prev + rep_d(bb) * d_c
        for h in heads:
            d_n = jnp.sum(kw[h * CHUNK:(h + 1) * CHUNK], axis=0, keepdims=True)
            n_ref[h] = a[h] * n_prev[h] + bb[h] * d_n
            m_ref[h] = m_new[h]

    def chunk(c, _):
        gla_chunk(c)
        mlstm_chunk(c)
        return 0

    lax.fori_loop(0, tt // CHUNK, chunk, 0, unroll=8)
    halo_ref[...] = qk_ref[tt - halo:tt, :]


def recurrent_mixers(u_mm, u_fx, gla_w_gate_up, gla_b_gate, gla_gain, conv_w, conv_b, b_i, b_f, ml_gain,
                     batch, seq, tt=1024):
    w = BRANCH_WIDTH
    dkh = N_HEADS * GLA_DK
    tt = min(tt, seq)
    nt = seq // tt
    wgu = (jnp.zeros((LANES, dkh), F32).at[GATE_GLA_LOW:GATE_GLA_LOW + GLA_GATE_RANK].set(gla_w_gate_up)
           .astype(BF16))
    gate_bias = (jnp.zeros((1, LANES), F32).at[0, GATE_ML_I:GATE_ML_I + N_HEADS].set(b_i)
                 .at[0, GATE_ML_F:GATE_ML_F + N_HEADS].set(b_f))
    const = lambda b, t: (0, 0)
    rows = lambda width, col0: pl.BlockSpec((tt, width), lambda b, t: (b * nt + t, col0 // width))
    out = jax.ShapeDtypeStruct((batch * seq, w), BF16)
    return pl.pallas_call(
        functools.partial(_recurrent_mixers_kernel, tt=tt),
        grid=(batch, nt),
        in_specs=[
            rows(LANES, FX_GATES),
            rows(dkh, MM_GLA_Q), rows(dkh, MM_GLA_K), rows(w, MM_GLA_V), rows(w, FX_GLA_R),
            pl.BlockSpec((LANES, dkh), const), pl.BlockSpec((1, dkh), const), pl.BlockSpec((1, w), const),
            rows(2 * w, FX_ML_QK), rows(w, MM_ML_V), rows(w, FX_ML_O),
            pl.BlockSpec((MLSTM_CONV, 2 * w), const), pl.BlockSpec((1, 2 * w), const),
            pl.BlockSpec((1, LANES), const), pl.BlockSpec((1, w), const),
        ],
        out_specs=[rows(w, 0), rows(w, 0)],
        out_shape=[out, out],
        scratch_shapes=[pltpu.VMEM((w, dkh), F32),
                        pltpu.VMEM((N_HEADS * HEAD_DV, HEAD_DV), F32),
                        pltpu.VMEM((N_HEADS, 1, HEAD_DV), F32),
                        pltpu.VMEM((N_HEADS, 1, 1), F32),
                        pltpu.VMEM((SUBLANES, 2 * w), F32)],
        compiler_params=_params("parallel", "arbitrary"),
        name="recurrent_mixers",
    )(u_fx,
      u_mm, u_mm, u_mm, u_fx, wgu, gla_b_gate.reshape(1, dkh), gla_gain.reshape(1, w),
      u_fx, u_mm, u_fx, conv_w, conv_b.reshape(1, 2 * w), gate_bias, ml_gain.reshape(1, w))


BAND_Q_CHUNKS = 4
BAND_TQ = BAND_Q_CHUNKS * CHUNK
BAND_TK = (CA_LEFT_CHUNKS + BAND_Q_CHUNKS) * CHUNK
BAND_PAD = CA_LEFT_CHUNKS * CHUNK


def _band_bias_kernel(table_ref, o_ref):
    l = pl.program_id(0)
    h = pl.program_id(1)
    row = lax.broadcasted_iota(jnp.int32, (BAND_TQ, BAND_TK), 0)
    col = lax.broadcasted_iota(jnp.int32, (BAND_TQ, BAND_TK), 1)
    idx = jnp.clip(row - col + BAND_PAD, -REL_CLIP, REL_CLIP) + REL_CLIP
    delta = row // CHUNK + CA_LEFT_CHUNKS - col // CHUNK
    in_band = (delta >= 0) & (delta <= CA_LEFT_CHUNKS)

    table = table_ref[0, 0]
    tiles = []
    for c0 in range(0, BAND_TK, LANES):
        i = idx[:, c0:c0 + LANES]
        val = jnp.zeros(i.shape, F32)
        for ch in range(table.shape[0]):
            src = jnp.broadcast_to(table[ch:ch + 1, :], i.shape)
            looked_up = jnp.take_along_axis(src, i & (LANES - 1), axis=1)
            val = jnp.where(i // LANES == ch, looked_up, val)
        tiles.append(val)
    o_ref[0, 0] = jnp.where(in_band, jnp.concatenate(tiles, axis=1), NEG)


def band_bias(rel_bias):
    depth, heads, n = rel_bias.shape
    chunks = -(-n // LANES)
    table = jnp.pad(rel_bias, ((0, 0), (0, 0), (0, chunks * LANES - n))).reshape(depth, heads, chunks, LANES)
    return pl.pallas_call(
        _band_bias_kernel,
        grid=(depth, N_HEADS),
        in_specs=[pl.BlockSpec((1, 1, chunks, LANES), lambda l, h: (l, h, 0, 0))],
        out_specs=pl.BlockSpec((1, 1, BAND_TQ, BAND_TK), lambda l, h: (l, h, 0, 0)),
        out_shape=jax.ShapeDtypeStruct((depth, N_HEADS, BAND_TQ, BAND_TK), F32),
        compiler_params=_params("parallel", "parallel"),
        name="band_bias",
    )(table)


def _band_attn_kernel(bias_ref, q_ref, k_ref, v_ref, o_ref, kp_ref, vp_ref, *, seq):
    kp_ref[0:BAND_PAD, :] = jnp.zeros((BAND_PAD, HEAD_DV), BF16)
    vp_ref[0:BAND_PAD, 0:HEAD_DV] = jnp.zeros((BAND_PAD, HEAD_DV), BF16)
    kp_ref[BAND_PAD:BAND_PAD + seq, :] = k_ref[...]
    vp_ref[BAND_PAD:BAND_PAD + seq, 0:HEAD_DV] = v_ref[...]
    vp_ref[:, HEAD_DV:] = jnp.ones((BAND_PAD + seq, HEAD_DV), BF16)
    scale = HEAD_DV ** -0.5

    def tile(t, starts_in_pad):
        r0 = t * BAND_TQ
        r0 = r0 if starts_in_pad else pl.multiple_of(r0, BAND_TQ)
        q = q_ref[pl.ds(r0, BAND_TQ), :]
        kb = kp_ref[pl.ds(r0, BAND_TK), :]
        vb = vp_ref[pl.ds(r0, BAND_TK), :]
        s = _nt(q, kb) * scale + bias_ref[0, 0]
        if starts_in_pad:
            col = lax.broadcasted_iota(jnp.int32, (BAND_TQ, BAND_TK), 1)
            s = jnp.where(col + r0 >= BAND_PAD, s, NEG)
        p = jnp.exp(s - jnp.max(s, axis=-1, keepdims=True))
        ov = _mm(p.astype(BF16), vb)
        o_ref[pl.ds(r0, BAND_TQ), :] = (ov[:, :HEAD_DV] / ov[:, HEAD_DV:]).astype(o_ref.dtype)

    n_pad_tiles = BAND_PAD // BAND_TQ
    for t in range(n_pad_tiles):
        tile(t, True)

    def body(t, _):
        tile(t, False)
        return 0

    lax.fori_loop(n_pad_tiles, seq // BAND_TQ, body, 0, unroll=7)


def band_attention(u_mm, bias, layer, batch, seq):
    cq, ck, cv = MM_CA_Q // LANES, MM_CA_K // LANES, MM_CA_V // LANES
    kern = functools.partial(_band_attn_kernel, seq=seq)
    return pl.pallas_call(
        kern,
        grid=(batch, N_HEADS),
        in_specs=[
            pl.BlockSpec((1, 1, BAND_TQ, BAND_TK), lambda b, h: (layer, h, 0, 0)),
            pl.BlockSpec((seq, LANES), lambda b, h: (b, cq + h)),
            pl.BlockSpec((seq, LANES), lambda b, h: (b, ck + h)),
            pl.BlockSpec((seq, LANES), lambda b, h: (b, cv + h)),
        ],
        out_specs=pl.BlockSpec((seq, HEAD_DV), lambda b, h: (b, h)),
        out_shape=jax.ShapeDtypeStruct((batch * seq, BRANCH_WIDTH), BF16),
        scratch_shapes=[pltpu.VMEM((BAND_PAD + seq, HEAD_DV), BF16),
                        pltpu.VMEM((BAND_PAD + seq, 2 * HEAD_DV), BF16)],
        compiler_params=_params("parallel", "parallel"),
        name="band_attention",
    )(bias, u_mm, u_mm, u_mm)


def _gated_merge_kernel(h_ref, ba_ref, bb_ref, bc_ref, bd_ref, wg_ref, bg_ref, wb_ref, o_ref):
    h = h_ref[...]
    merged = None
    for bi, br_ref in enumerate((ba_ref, bb_ref, bc_ref, bd_ref)):
        gate = _sigmoid(_mm(h, wg_ref[bi]) + bg_ref[bi])
        term = gate * _mm(br_ref[...], wb_ref[bi])
        merged = term if merged is None else merged + term
    o_ref[...] = merged.astype(o_ref.dtype)


def gated_merge(h, branches, w_gate, b_gate, w_branch, layer, tm=1024, tn=512):
    m, d = h.shape
    w = BRANCH_WIDTH
    tm = min(tm, m)
    row_d = pl.BlockSpec((tm, d), lambda i, j: (i, 0))
    row_w = pl.BlockSpec((tm, w), lambda i, j: (i, 0))
    return pl.pallas_call(
        _gated_merge_kernel,
        grid=(m // tm, d // tn),
        in_specs=[
            row_d, row_w, row_w, row_w, row_w,
            pl.BlockSpec((None, 4, d, tn), lambda i, j: (layer, 0, 0, j)),
            pl.BlockSpec((None, 4, 1, tn), lambda i, j: (layer, 0, 0, j)),
            pl.BlockSpec((None, 4, w, tn), lambda i, j: (layer, 0, 0, j)),
        ],
        out_specs=pl.BlockSpec((tm, tn), lambda i, j: (i, j)),
        out_shape=jax.ShapeDtypeStruct((m, d), BF16),
        compiler_params=_params("parallel", "arbitrary"),
        name="gated_merge",
    )(h, *branches, w_gate, b_gate.reshape(-1, 4, 1, d), w_branch)


def _out_proj_kernel(a_ref, x_ref, wo_ref, gpost_ref, gnext_ref, xo_ref, ho_ref, *, sub):
    for r in range(a_ref.shape[0] // sub):
        rows = slice(r * sub, (r + 1) * sub)
        x_new = x_ref[rows, :] + _rms(_mm(a_ref[rows, :], wo_ref[...]), gpost_ref[...])
        xo_ref[rows, :] = x_new
        ho_ref[rows, :] = _rms(x_new, gnext_ref[...]).astype(ho_ref.dtype)


def out_proj(a, x, w_out, layer, g_post, g_next, tm=512, sub=128):
    m, d = x.shape
    row_d = pl.BlockSpec((tm, d), lambda i: (i, 0))
    vec_d = pl.BlockSpec((1, d), lambda i: (0, 0))
    return pl.pallas_call(
        functools.partial(_out_proj_kernel, sub=min(sub, tm)),
        grid=(m // tm,),
        in_specs=[row_d, row_d, pl.BlockSpec((None, d, d), lambda i: (layer, 0, 0)), vec_d, vec_d],
        out_specs=[row_d, row_d],
        out_shape=[jax.ShapeDtypeStruct((m, d), F32), jax.ShapeDtypeStruct((m, d), BF16)],
        compiler_params=_params("parallel"),
        name="out_proj",
    )(a, x, w_out, g_post.reshape(1, d), g_next.reshape(1, d))


def _ffn_kernel(h_ref, x_hbm, wu_ref, wd_ref, gpost_ref, *rest, emit_next, sub):
    if emit_next:
        gnext_ref, xo_ref, ho_ref, x_ref, x_sem = rest
    else:
        xo_ref, x_ref, x_sem = rest
    i = pl.program_id(0)
    j = pl.program_id(1)
    tm = x_ref.shape[0]

    def x_copy():
        return pltpu.make_async_copy(x_hbm.at[pl.ds(pl.multiple_of(i * tm, tm), tm)], x_ref, x_sem)

    @pl.when(j == 0)
    def _():
        x_copy().start()
        xo_ref[...] = jnp.zeros_like(xo_ref)

    for s in range(wu_ref.shape[1] // sub):
        cols = slice(s * sub, (s + 1) * sub)
        act = jnp.square(jnp.maximum(_mm(h_ref[...], wu_ref[:, cols]), 0.0))
        xo_ref[...] += _mm(act.astype(BF16), wd_ref[cols, :])

    @pl.when(j == pl.num_programs(1) - 1)
    def _():
        x_copy().wait()
        x_new = x_ref[...] + _rms(xo_ref[...], gpost_ref[...])
        xo_ref[...] = x_new
        if emit_next:
            ho_ref[...] = _rms(x_new, gnext_ref[...]).astype(ho_ref.dtype)


def ffn(h, x, w_up, w_down, layer, g_post, g_next=None, tm=1024, tf=1024, sub=512):
    m, d = x.shape
    f = w_up.shape[2]
    tm = min(tm, m)
    emit_next = g_next is not None
    row_d = pl.BlockSpec((tm, d), lambda i, j: (i, 0))
    vec_d = pl.BlockSpec((1, d), lambda i, j: (0, 0))
    in_specs = [row_d, pl.BlockSpec(memory_space=pl.ANY), pl.BlockSpec((None, d, tf), lambda i, j: (layer, 0, j)),
                pl.BlockSpec((None, tf, d), lambda i, j: (layer, j, 0)), vec_d]
    args = [h, x, w_up, w_down, g_post.reshape(1, d)]
    out_specs = [row_d]
    out_shape = [jax.ShapeDtypeStruct((m, d), F32)]
    if emit_next:
        in_specs.append(vec_d)
        args.append(g_next.reshape(1, d))
        out_specs.append(row_d)
        out_shape.append(jax.ShapeDtypeStruct((m, d), BF16))
    return pl.pallas_call(
        functools.partial(_ffn_kernel, emit_next=emit_next, sub=sub),
        grid=(m // tm, f // tf),
        in_specs=in_specs,
        out_specs=out_specs,
        out_shape=out_shape,
        scratch_shapes=[pltpu.VMEM((tm, d), F32), pltpu.SemaphoreType.DMA(())],
        compiler_params=pltpu.CompilerParams(dimension_semantics=("parallel", "arbitrary"),
                                             vmem_limit_bytes=FFN_VMEM_LIMIT),
        name="ffn",
    )(*args)


def _pack_w_in(w_in_l):
    w = BRANCH_WIDTH
    gla0 = 3 * w
    ml0 = gla0 + 2 * N_HEADS * GLA_DK + w + GLA_GATE_RANK + w
    ca0 = ml0 + 4 * w + 2 * N_HEADS
    c = lambda a, b: w_in_l[:, a:b]
    g_low0 = gla0 + 2 * N_HEADS * GLA_DK + w
    gates = jnp.concatenate([c(g_low0, g_low0 + GLA_GATE_RANK), c(ml0 + 4 * w, ml0 + 4 * w + 2 * N_HEADS)], axis=1)
    gates = jnp.pad(gates, ((0, 0), (0, LANES - gates.shape[1])))
    w_mm = jnp.concatenate([c(0, 3 * w), c(gla0, gla0 + 2 * N_HEADS * GLA_DK + w),
                            c(ml0 + 2 * w, ml0 + 3 * w), c(ca0, ca0 + 3 * w)], axis=1)
    w_fx = jnp.concatenate([c(ml0, ml0 + 2 * w), c(g_low0 + GLA_GATE_RANK, g_low0 + GLA_GATE_RANK + w),
                            c(ml0 + 3 * w, ml0 + 4 * w), gates], axis=1)
    return w_mm, w_fx


def kernel(x, norm_mix_pre, norm_mix_post, w_in, diff_lambda, diff_norm, gla_w_gate_up, gla_b_gate, gla_norm,
           mlstm_conv_w, mlstm_conv_b, mlstm_b_i, mlstm_b_f, mlstm_norm, rel_bias, w_branch, w_gate, b_gate,
           w_out, norm_ffn_pre, norm_ffn_post, w_up, w_down):
    batch, seq, d = x.shape
    depth = w_in.shape[0]
    m = batch * seq
    xf = x.reshape(m, d)
    bias_tiles = band_bias(rel_bias)
    w_gate, w_branch, w_out, w_up, w_down = (a.astype(BF16) for a in (w_gate, w_branch, w_out, w_up, w_down))
    h = rmsnorm_cast(xf, norm_mix_pre[0])
    for l in range(depth):
        w_mm, w_fx = _pack_w_in(w_in[l].astype(BF16))
        u_mm = matmul(h, w_mm, BF16, tm=min(1024, m), tn=MM_COLS // 2, name="in_proj_mm")
        u_fx = matmul(h, w_fx, F32, tm=min(1024, m), tn=FX_COLS, name="in_proj_fx")
        o_gla, o_mlstm = recurrent_mixers(u_mm, u_fx, gla_w_gate_up[l], gla_b_gate[l], gla_norm[l], mlstm_conv_w[l],
                                          mlstm_conv_b[l], mlstm_b_i[l], mlstm_b_f[l], mlstm_norm[l], batch, seq)
        branches = (
            diff_attention(u_mm, diff_lambda[l], diff_norm[l], l, batch, seq),
            o_gla,
            o_mlstm,
            band_attention(u_mm, bias_tiles, l, batch, seq),
        )
        merged = gated_merge(h, branches, w_gate, b_gate, w_branch, l)
        xf, h2 = out_proj(merged, xf, w_out, l, norm_mix_post[l], norm_ffn_pre[l])
        g_next = norm_mix_pre[l + 1] if l + 1 < depth else None
        outs = ffn(h2, xf, w_up, w_down, l, norm_ffn_post[l], g_next)
        xf, h = outs if g_next is not None else (outs[0], None)
    return xf.reshape(batch, seq, d)
```
